```python
import jax
import jax.numpy as jnp
from jax import lax
import numpy as np

D_MODEL = 1024
BATCH = 8
SEQ = 2048
DEPTH = 1

GRID_W = 64
CTX_LEN = 256
POOL_GROUPS = 4
POOL_WINDOWS = (2, 4, 8, 16)
POOL_WIDTH = D_MODEL // 2
POOL_GROUP_DIM = POOL_WIDTH // POOL_GROUPS
RET_HEADS = 8
RET_QK_DIM = D_MODEL // 16
RET_V_DIM = D_MODEL // 8
RET_QK_WIDTH = RET_HEADS * RET_QK_DIM
RET_V_WIDTH = RET_HEADS * RET_V_DIM
RET_CHUNK = 128
K_SCALE = RET_QK_DIM ** -0.5
ROPE_BASE = 10000.0
D_FF = ((8 * D_MODEL + 3 * 256 - 1) // (3 * 256)) * 256
EPS = 1e-6

P_OFF = 0
Q_OFF = P_OFF + POOL_WIDTH
K_OFF = Q_OFF + RET_QK_WIDTH
V_OFF = K_OFF + RET_QK_WIDTH
G_OFF = V_OFF + RET_V_WIDTH
GA_OFF = G_OFF + RET_V_WIDTH
GB_OFF = GA_OFF + D_MODEL
IN_WIDTH = GB_OFF + D_MODEL

kernel_name = 'hybrid_pool_retention_dit_block'


def rms_norm(x, gain):
    xf = x.astype(jnp.float32)
    y = xf * lax.rsqrt(jnp.mean(xf * xf, axis=-1, keepdims=True) + EPS)
    return (y * gain.astype(jnp.float32)).astype(x.dtype)


def modulate(h, shift, scale):
    return h * (1.0 + scale) + shift


def swiglu(h, w1, w3, w2):
    return (jax.nn.silu(h @ w1) * (h @ w3)) @ w2


def centred_box_mean(u, window, axis):
    n = u.shape[axis]
    cs = jnp.cumsum(u.astype(jnp.float32), axis=axis)
    pad = [(0, 0)] * u.ndim
    pad[axis] = (1, 0)
    cs = jnp.pad(cs, pad)
    pos = jnp.arange(n)
    lo = jnp.clip(pos - window // 2, 0, n)
    hi = jnp.clip(pos + (window - window // 2), 0, n)
    total = jnp.take(cs, hi, axis=axis) - jnp.take(cs, lo, axis=axis)
    shape = [1] * u.ndim
    shape[axis] = n
    count = (hi - lo).astype(jnp.float32).reshape(shape)
    return (total / count).astype(u.dtype)


def pool_mixer(u, rows, w_pool, pool_scale):
    B, L, _ = u.shape
    groups = jnp.split(u, POOL_GROUPS, axis=-1)
    diffs = []
    for g, w in enumerate(POOL_WINDOWS):
        ug = groups[g]
        if rows is None:
            m = centred_box_mean(ug, w, 1)
        else:
            grid = ug.reshape(B, rows, GRID_W, POOL_GROUP_DIM)
            m = centred_box_mean(centred_box_mean(grid, w, 2), w, 1).reshape(B, L, POOL_GROUP_DIM)
        diffs.append(m - ug)
    d = jnp.stack(diffs, axis=2)
    y = jnp.einsum('blgc,gcd->blgd', d, w_pool)
    return y.reshape(B, L, POOL_WIDTH) * pool_scale


def rope_2d(length, dtype):
    t = jnp.arange(length)
    row = (t // GRID_W).astype(jnp.float32)
    col = (t % GRID_W).astype(jnp.float32)
    n_freq = RET_QK_DIM // 4
    inv_freq = ROPE_BASE ** (-jnp.arange(n_freq, dtype=jnp.float32) / n_freq)
    ang = jnp.concatenate([row[:, None] * inv_freq, col[:, None] * inv_freq], axis=-1)
    return jnp.cos(ang).astype(dtype), jnp.sin(ang).astype(dtype)


def apply_rope(t, cos, sin):
    half = t.shape[-1] // 2
    t1, t2 = t[..., :half], t[..., half:]
    cos = cos[None, :, None, :]
    sin = sin[None, :, None, :]
    return jnp.concatenate([t1 * cos - t2 * sin, t1 * sin + t2 * cos], axis=-1)


def chunk_retention(q, k, v, log_gamma, s0, include_diag):
    B, L, H, _ = q.shape
    Dv = v.shape[-1]
    C = RET_CHUNK
    n = L // C
    lg = log_gamma.astype(jnp.float32)
    idx = jnp.arange(C, dtype=jnp.float32)
    diff = idx[:, None] - idx[None, :]
    keep = (diff >= 0) if include_diag else (diff > 0)
    intra = jnp.where(keep[None], jnp.exp(lg[:, None, None] * jnp.maximum(diff, 0.0)[None]), 0.0).astype(q.dtype)
    q_dec = jnp.exp(lg[:, None] * (idx + 1.0)).astype(q.dtype)
    k_dec = jnp.exp(lg[:, None] * (C - 1.0 - idx)).astype(q.dtype)
    c_dec = jnp.exp(lg * C).astype(q.dtype)

    def to_chunks(t):
        return t.reshape(B, n, C, H, t.shape[-1]).transpose(1, 0, 3, 2, 4)

    def step(state, chunk):
        qi, ki, vi = chunk
        scores = jnp.einsum('bhqd,bhkd->bhqk', qi, ki) * intra[None]
        out = (jnp.einsum('bhqk,bhkv->bhqv', scores, vi)
               + jnp.einsum('bhqd,bhdv->bhqv', qi * q_dec[None, :, :, None], state))
        state = (state * c_dec[None, :, None, None]
                 + jnp.einsum('bhkd,bhkv->bhdv', ki * k_dec[None, :, :, None], vi))
        return state, out

    _, out = lax.scan(step, s0, (to_chunks(q), to_chunks(k), to_chunks(v)))
    return out.transpose(1, 0, 3, 2, 4).reshape(B, L, H, Dv)


def bidir_retention(q, k, v, lg_f, lg_b, s_f, s_b):
    y_f = chunk_retention(q, k, v, lg_f, s_f, True)
    y_b = chunk_retention(jnp.flip(q, 1), jnp.flip(k, 1), jnp.flip(v, 1), lg_b, s_b, False)
    return y_f + jnp.flip(y_b, 1)


def context_final_states(k, v, lg_f, lg_b):
    Lc = k.shape[1]
    pos = jnp.arange(Lc, dtype=jnp.float32)
    w_f = jnp.exp(lg_f.astype(jnp.float32)[:, None] * (Lc - 1.0 - pos)).astype(k.dtype)
    w_b = jnp.exp(lg_b.astype(jnp.float32)[:, None] * pos).astype(k.dtype)
    s_f = jnp.einsum('hl,blhd,blhv->bhdv', w_f, k, v)
    s_b = jnp.einsum('hl,blhd,blhv->bhdv', w_b, k, v)
    return s_f, s_b


def head_group_norm(y, gain):
    B, L, H, Dv = y.shape
    yf = y.astype(jnp.float32)
    mu = jnp.mean(yf, axis=-1, keepdims=True)
    var = jnp.mean(jnp.square(yf - mu), axis=-1, keepdims=True)
    yn = ((yf - mu) * lax.rsqrt(var + EPS)).reshape(B, L, H * Dv)
    return (yn * gain.astype(jnp.float32)).astype(y.dtype)


def token_mixers(z, rows, rope, s_f, s_b, lg_f, lg_b, w_pool, pool_scale, gn_w, w_pa, w_rb, w_o):
    B, L, _ = z.shape
    u = z[..., P_OFF:Q_OFF]
    q = z[..., Q_OFF:K_OFF].reshape(B, L, RET_HEADS, RET_QK_DIM)
    k = z[..., K_OFF:V_OFF].reshape(B, L, RET_HEADS, RET_QK_DIM) * K_SCALE
    v = z[..., V_OFF:G_OFF].reshape(B, L, RET_HEADS, RET_V_DIM)
    g = z[..., G_OFF:GA_OFF]
    gate_a = jax.nn.sigmoid(z[..., GA_OFF:GB_OFF])
    gate_b = jax.nn.sigmoid(z[..., GB_OFF:IN_WIDTH])
    branch_a = pool_mixer(u, rows, w_pool, pool_scale) @ w_pa
    if rope is not None:
        q = apply_rope(q, rope[0], rope[1])
        k = apply_rope(k, rope[0], rope[1])
    y = bidir_retention(q, k, v, lg_f, lg_b, s_f, s_b)
    y = head_group_norm(y, gn_w) * jax.nn.silu(g)
    branch_b = y @ w_rb
    return (gate_a * branch_a + gate_b * branch_b) @ w_o


def setup_inputs(seed: int = 0) -> dict:
    key = jax.random.key(seed)
    ks = jax.random.split(key, 21)

    def nrm(k, shape, scale):
        return jax.random.normal(k, shape, jnp.float32) * scale

    decay_logit = jnp.log(2.0 ** (5.0 + jnp.arange(RET_HEADS, dtype=jnp.float32)) - 1.0)
    return {
        'x': nrm(ks[0], (BATCH, SEQ, D_MODEL), 1.0),
        'c': nrm(ks[1], (BATCH, D_MODEL), 1.0),
        'ctx': nrm(ks[2], (BATCH, CTX_LEN, D_MODEL), 1.0),
        'c_ctx': nrm(ks[3], (D_MODEL,), 1.0),
        'w_ada': nrm(ks[4], (DEPTH, D_MODEL, 6 * D_MODEL), 0.5 * D_MODEL ** -0.5),
        'b_ada': nrm(ks[5], (DEPTH, 6 * D_MODEL), 0.01),
        'norm_mix': 1.0 + nrm(ks[6], (DEPTH, D_MODEL), 0.05),
        'norm_ffn': 1.0 + nrm(ks[7], (DEPTH, D_MODEL), 0.05),
        'w_in': nrm(ks[8], (DEPTH, D_MODEL, IN_WIDTH), D_MODEL ** -0.5),
        'w_pool': nrm(ks[9], (DEPTH, POOL_GROUPS, POOL_GROUP_DIM, POOL_GROUP_DIM), POOL_GROUP_DIM ** -0.5),
        'pool_scale': 1.0 + nrm(ks[10], (DEPTH, POOL_WIDTH), 0.1),
        'ret_decay_f': decay_logit + nrm(ks[11], (DEPTH, RET_HEADS), 0.1),
        'ret_decay_b': decay_logit + nrm(ks[12], (DEPTH, RET_HEADS), 0.1),
        'ret_gn_w': 1.0 + nrm(ks[13], (DEPTH, RET_V_WIDTH), 0.05),
        'w_pa': nrm(ks[14], (DEPTH, POOL_WIDTH, D_MODEL), POOL_WIDTH ** -0.5),
        'w_rb': nrm(ks[15], (DEPTH, RET_V_WIDTH, D_MODEL), RET_V_WIDTH ** -0.5),
        'w_o': nrm(ks[16], (DEPTH, D_MODEL, D_MODEL), D_MODEL ** -0.5),
        'w_ff1': nrm(ks[17], (DEPTH, D_MODEL, D_FF), D_MODEL ** -0.5),
        'w_ff3': nrm(ks[18], (DEPTH, D_MODEL, D_FF), D_MODEL ** -0.5),
        'w_ff2': nrm(ks[19], (DEPTH, D_FF, D_MODEL), D_FF ** -0.5),
        'norm_final': 1.0 + nrm(ks[20], (D_MODEL,), 0.05),
    }


def reference(x, c, ctx, c_ctx, w_ada, b_ada, norm_mix, norm_ffn, w_in, w_pool, pool_scale,
              ret_decay_f, ret_decay_b, ret_gn_w, w_pa, w_rb, w_o, w_ff1, w_ff3, w_ff2, norm_final):
    B, L, _ = x.shape
    rows = L // GRID_W
    rope = rope_2d(L, x.dtype)
    silu_c = jax.nn.silu(c)
    silu_cc = jax.nn.silu(c_ctx)
    for l in range(DEPTH):
        update_ctx = l < DEPTH - 1
        mod = silu_c @ w_ada[l] + b_ada[l]
        sh_m, sc_m, g_m, sh_f, sc_f, g_f = [m[:, None, :] for m in jnp.split(mod, 6, axis=-1)]
        mod_c = silu_cc @ w_ada[l] + b_ada[l]
        shc_m, scc_m, gc_m, shc_f, scc_f, gc_f = jnp.split(mod_c, 6, axis=-1)
        lg_f = jax.nn.log_sigmoid(ret_decay_f[l])
        lg_b = jax.nn.log_sigmoid(ret_decay_b[l])
        mixer_params = (lg_f, lg_b, w_pool[l], pool_scale[l], ret_gn_w[l], w_pa[l], w_rb[l], w_o[l])

        hc = modulate(rms_norm(ctx, norm_mix[l]), shc_m, scc_m)
        if update_ctx:
            zc = hc @ w_in[l]
            kc_raw, vc_raw = zc[..., K_OFF:V_OFF], zc[..., V_OFF:G_OFF]
        else:
            kvc = hc @ w_in[l][:, K_OFF:G_OFF]
            kc_raw, vc_raw = kvc[..., :RET_QK_WIDTH], kvc[..., RET_QK_WIDTH:]
        Lc = ctx.shape[1]
        kc = kc_raw.reshape(B, Lc, RET_HEADS, RET_QK_DIM) * K_SCALE
        vc = vc_raw.reshape(B, Lc, RET_HEADS, RET_V_DIM)
        s_f, s_b = context_final_states(kc, vc, lg_f, lg_b)

        h = modulate(rms_norm(x, norm_mix[l]), sh_m, sc_m)
        x = x + g_m * token_mixers(h @ w_in[l], rows, rope, s_f, s_b, *mixer_params)
        h = modulate(rms_norm(x, norm_ffn[l]), sh_f, sc_f)
        x = x + g_f * swiglu(h, w_ff1[l], w_ff3[l], w_ff2[l])

        if update_ctx:
            zero_state = jnp.zeros((B, RET_HEADS, RET_QK_DIM, RET_V_DIM), ctx.dtype)
            ctx = ctx + gc_m * token_mixers(zc, None, None, zero_state, zero_state, *mixer_params)
            hc = modulate(rms_norm(ctx, norm_ffn[l]), shc_f, scc_f)
            ctx = ctx + gc_f * swiglu(hc, w_ff1[l], w_ff3[l], w_ff2[l])
    return rms_norm(x, norm_final)
```

```python
import functools

import numpy as np
import jax
import jax.numpy as jnp
from jax import lax
from jax.experimental import pallas as pl
from jax.experimental.pallas import tpu as pltpu

F32 = jnp.float32
BF16 = jnp.bfloat16

GRID_W = 64
POOL_WINDOWS = (2, 4, 8, 16)
RET_HEADS = 8
RET_CHUNK = 128
ROPE_BASE = 10000.0
EPS = 1e-6

LANES = 128
MOD_ROWS = 16
VMEM_LIMIT_BYTES = 56 * 1024 * 1024

TOKEN_TILE = 512
COL_TILE = 512
POOL_BLOCK = 256
FFN_TILE = 256


def _silu(t):
    return t * jax.nn.sigmoid(t)


def _log_sigmoid(t):
    return jnp.minimum(t, 0.0) - jnp.log1p(jnp.exp(-jnp.abs(t)))


def _rms_modulate(x, gain, shift, scale):
    y = x * lax.rsqrt(jnp.mean(x * x, axis=-1, keepdims=True) + EPS) * gain
    return y * (1.0 + scale) + shift


def _dot(a, b):
    return jnp.dot(a, b, preferred_element_type=F32)


def _params(*semantics):
    return pltpu.CompilerParams(dimension_semantics=semantics, vmem_limit_bytes=VMEM_LIMIT_BYTES)


def _mod_kernel(c_ref, w_ref, b_ref, o_ref):
    s = _silu(c_ref[...]).astype(BF16)
    o_ref[...] = _dot(s, w_ref[...].astype(BF16)) + b_ref[...]


def _modulation(cc, w_ada, b_ada):
    d, n = w_ada.shape
    tn = d
    return pl.pallas_call(
        _mod_kernel,
        grid=(n // tn,),
        in_specs=[
            pl.BlockSpec((MOD_ROWS, d), lambda j: (0, 0)),
            pl.BlockSpec((d, tn), lambda j: (0, j)),
            pl.BlockSpec((1, tn), lambda j: (0, j)),
        ],
        out_specs=pl.BlockSpec((MOD_ROWS, tn), lambda j: (0, j)),
        out_shape=jax.ShapeDtypeStruct((MOD_ROWS, n), F32),
        compiler_params=_params("arbitrary"),
        name="modulation",
    )(cc, w_ada, b_ada.reshape(1, n))


def _ctx_kernel(ctx_ref, sh_ref, sc_ref, gain_ref, wkv_ref, decf_ref, decb_ref, o_ref, *, k_scale):
    lc = ctx_ref.shape[1]
    qk_width = wkv_ref.shape[1] // 3
    h = _rms_modulate(ctx_ref[0], gain_ref[...], sh_ref[0], sc_ref[0]).astype(BF16)
    kv = _dot(h, wkv_ref[...])
    pos = lax.broadcasted_iota(jnp.int32, (lc, LANES), 0).astype(F32)
    lane = lax.broadcasted_iota(jnp.int32, (1, LANES), 1)
    row = lax.broadcasted_iota(jnp.int32, (2 * LANES, 2 * LANES), 0)
    col = lax.broadcasted_iota(jnp.int32, (2 * LANES, 2 * LANES), 1)
    same_head = ((row % LANES) < LANES // 2) == (col < LANES)
    for p in range(RET_HEADS // 2):
        k = kv[:, LANES * p:LANES * (p + 1)] * k_scale
        v = kv[:, qk_width + 2 * LANES * p:qk_width + 2 * LANES * (p + 1)].astype(BF16)
        lgf = jnp.where(lane < LANES // 2, _log_sigmoid(decf_ref[2 * p]), _log_sigmoid(decf_ref[2 * p + 1]))
        lgb = jnp.where(lane < LANES // 2, _log_sigmoid(decb_ref[2 * p]), _log_sigmoid(decb_ref[2 * p + 1]))
        kf = k * jnp.exp(lgf * (lc - 1.0 - pos))
        kb = k * jnp.exp(lgb * pos)
        kk = jnp.concatenate([kf, kb], axis=1).astype(BF16)
        st = lax.dot_general(kk, v, (((0,), (0,)), ((), ())), preferred_element_type=F32)
        o_ref[0, p] = jnp.where(same_head, st, 0.0)


def _ctx_states(ctx, mod3, gain, wkv, decf, decb, k_scale):
    b, lc, d = ctx.shape
    pairs = RET_HEADS // 2
    ctx_row = MOD_ROWS // 2
    return pl.pallas_call(
        functools.partial(_ctx_kernel, k_scale=k_scale),
        grid=(b,),
        in_specs=[
            pl.BlockSpec((1, lc, d), lambda i: (i, 0, 0)),
            pl.BlockSpec((1, 1, d), lambda i: (ctx_row, 0, 0)),
            pl.BlockSpec((1, 1, d), lambda i: (ctx_row, 0, 1)),
            pl.BlockSpec((1, d), lambda i: (0, 0)),
            pl.BlockSpec(wkv.shape, lambda i: (0, 0)),
            pl.BlockSpec((RET_HEADS, 1, LANES), lambda i: (0, 0, 0)),
            pl.BlockSpec((RET_HEADS, 1, LANES), lambda i: (0, 0, 0)),
        ],
        out_specs=pl.BlockSpec((1, pairs, 2 * LANES, 2 * LANES), lambda i: (i, 0, 0, 0)),
        out_shape=jax.ShapeDtypeStruct((b, pairs, 2 * LANES, 2 * LANES), F32),
        compiler_params=_params("arbitrary"),
        name="ctx_states",
    )(ctx, mod3, mod3, gain, wkv, decf, decb)


def _swap_halves(t):
    lane = lax.broadcasted_iota(jnp.int32, (1, LANES), 1)
    first_half = (lane % (LANES // 2)) < LANES // 4
    parts = []
    for i in range(t.shape[1] // LANES):
        s = t[:, LANES * i:LANES * (i + 1)]
        ahead = pltpu.roll(s, LANES - LANES // 4, axis=1)
        behind = pltpu.roll(s, LANES // 4, axis=1)
        parts.append(jnp.where(first_half, ahead, behind))
    return jnp.concatenate(parts, axis=1)


def _inproj_kernel(x_ref, sh_ref, sc_ref, gain_ref, w_ref, cos_ref, sin_ref, *out_refs, sections, k_scale):
    h = _rms_modulate(x_ref[0], gain_ref[...], sh_ref[0], sc_ref[0]).astype(BF16)
    for (kind, lo, hi), o_ref in zip(sections, out_refs):
        for c0 in range(lo, hi, COL_TILE):
            acc = _dot(h, w_ref[:, c0:c0 + COL_TILE])
            if kind == "q" or kind == "k":
                if kind == "k":
                    acc = acc * k_scale
                acc = acc * cos_ref[...] + _swap_halves(acc) * sin_ref[...]
            elif kind == "silu":
                acc = _silu(acc)
            elif kind == "sigmoid":
                acc = jax.nn.sigmoid(acc)
            o_ref[0, :, c0 - lo:c0 - lo + COL_TILE] = acc.astype(BF16)


def _in_proj(x, mod3, gain, w_in, cos_t, sin_t, sections, k_scale):
    b, l, d = x.shape
    n = w_in.shape[1]
    tm = TOKEN_TILE
    return pl.pallas_call(
        functools.partial(_inproj_kernel, sections=sections, k_scale=k_scale),
        grid=(l // tm, b),
        in_specs=[
            pl.BlockSpec((1, tm, d), lambda i, bi: (bi, i, 0)),
            pl.BlockSpec((1, 1, d), lambda i, bi: (bi, 0, 0)),
            pl.BlockSpec((1, 1, d), lambda i, bi: (bi, 0, 1)),
            pl.BlockSpec((1, d), lambda i, bi: (0, 0)),
            pl.BlockSpec((d, n), lambda i, bi: (0, 0), pipeline_mode=pl.Buffered(1)),
            pl.BlockSpec((tm, COL_TILE), lambda i, bi: (i, 0)),
            pl.BlockSpec((tm, COL_TILE), lambda i, bi: (i, 0)),
        ],
        out_specs=[pl.BlockSpec((1, tm, hi - lo), lambda i, bi: (bi, i, 0)) for _, lo, hi in sections],
        out_shape=[jax.ShapeDtypeStruct((b, l, hi - lo), BF16) for _, lo, hi in sections],
        compiler_params=_params("arbitrary", "arbitrary"),
        name="in_proj",
    )(x, mod3, mod3, gain, w_in, cos_t, sin_t)


def _pool_kernel(u_ref, acol_ref, invc_ref, wpool_ref, scale_ref, o_ref, cm_ref, d_ref):
    l = u_ref.shape[1]
    rows = l // GRID_W
    for g, w in enumerate(POOL_WINDOWS):
        gs = slice(LANES * g, LANES * (g + 1))
        for t0 in range(0, l, POOL_BLOCK):
            cs = _dot(acol_ref[g], u_ref[0, t0:t0 + POOL_BLOCK, gs])
            cm_ref[t0:t0 + POOL_BLOCK, :] = cs * invc_ref[g]
        for r in range(rows):
            lo = max(r - w // 2, 0)
            hi = min(r + w - w // 2, rows)
            acc = cm_ref[GRID_W * lo:GRID_W * (lo + 1), :]
            for rr in range(lo + 1, hi):
                acc = acc + cm_ref[GRID_W * rr:GRID_W * (rr + 1), :]
            m = acc / float(hi - lo)
            rs = slice(GRID_W * r, GRID_W * (r + 1))
            d_ref[rs, :] = (m - u_ref[0, rs, gs].astype(F32)).astype(BF16)
        y = _dot(d_ref[...], wpool_ref[g]) * scale_ref[:, gs]
        o_ref[0, :, gs] = y.astype(BF16)


def _pool(u, acol, invc, w_pool, pool_scale):
    b, l, width = u.shape
    gd = w_pool.shape[1]
    return pl.pallas_call(
        _pool_kernel,
        grid=(b,),
        in_specs=[
            pl.BlockSpec((1, l, width), lambda i: (i, 0, 0)),
            pl.BlockSpec(acol.shape, lambda i: (0, 0, 0)),
            pl.BlockSpec(invc.shape, lambda i: (0, 0, 0)),
            pl.BlockSpec(w_pool.shape, lambda i: (0, 0, 0)),
            pl.BlockSpec((1, width), lambda i: (0, 0)),
        ],
        out_specs=pl.BlockSpec((1, l, width), lambda i: (i, 0, 0)),
        out_shape=jax.ShapeDtypeStruct((b, l, width), BF16),
        scratch_shapes=[pltpu.VMEM((l, gd), F32), pltpu.VMEM((l, gd), BF16)],
        compiler_params=_params("arbitrary"),
        name="pool",
    )(u, acol, invc, w_pool, pool_scale)


def _ret_kernel(q_ref, k_ref, v_ref, sg_ref, st_ref, decf_ref, decb_ref, gn_ref, o_ref, a_ref, s_ref):
    c = RET_CHUNK
    l = q_ref.shape[1]
    n = l // c
    half = LANES // 2
    lane = lax.broadcasted_iota(jnp.int32, (1, LANES), 1)
    lgf0, lgf1 = _log_sigmoid(decf_ref[0]), _log_sigmoid(decf_ref[1])
    lgb0, lgb1 = _log_sigmoid(decb_ref[0]), _log_sigmoid(decb_ref[1])
    lgf = jnp.where(lane < half, lgf0, lgf1)
    lgb = jnp.where(lane < half, lgb0, lgb1)
    idx = lax.broadcasted_iota(jnp.int32, (c, LANES), 0).astype(F32)
    kdec_f = jnp.exp(lgf * (c - 1.0 - idx))
    kdec_b = jnp.exp(lgb * idx)
    qdec_f = jnp.exp(lgf * (idx + 1.0))
    qdec_b = jnp.exp(lgb * (c - idx))
    diff = idx - lax.broadcasted_iota(jnp.int32, (c, c), 1).astype(F32)

    def intra_mask(lf, lb):
        return jnp.where(diff >= 0.0, jnp.exp(lf * jnp.maximum(diff, 0.0)), jnp.exp(lb * jnp.maximum(-diff, 0.0)))

    mcat = jnp.concatenate([intra_mask(lgf0, lgb0), intra_mask(lgf1, lgb1)], axis=1)
    srow = lax.broadcasted_iota(jnp.int32, (LANES, 2 * LANES), 0)
    scol = lax.broadcasted_iota(jnp.int32, (LANES, 2 * LANES), 1)
    same_head = (srow < half) == (scol < LANES)
    two = lambda t: jnp.concatenate([t, t], axis=1)
    cdec_f = jnp.where(srow < half, two(jnp.exp(lgf0 * c)), two(jnp.exp(lgf1 * c)))
    cdec_b = jnp.where(srow < half, two(jnp.exp(lgb0 * c)), two(jnp.exp(lgb1 * c)))
    vlane = lax.broadcasted_iota(jnp.int32, (1, 2 * LANES), 1)

    for j in range(n):
        cs = slice(c * j, c * (j + 1))
        kj = k_ref[0, cs, :].astype(F32)
        kk = jnp.concatenate([kj * kdec_f, kj * kdec_b], axis=1).astype(BF16)
        a_ref[j] = lax.dot_general(kk, v_ref[0, cs, :], (((0,), (0,)), ((), ())), preferred_element_type=F32)

    sf = st_ref[0, 0, 0:LANES, :]
    for j in range(n):
        s_ref[j, 0:LANES, :] = jnp.where(same_head, sf, 0.0).astype(BF16)
        sf = sf * cdec_f + a_ref[j, 0:LANES, :]
    sb = st_ref[0, 0, LANES:2 * LANES, :]
    for j in reversed(range(n)):
        s_ref[j, LANES:2 * LANES, :] = jnp.where(same_head, sb, 0.0).astype(BF16)
        sb = sb * cdec_b + a_ref[j, LANES:2 * LANES, :]

    gn = gn_ref[...]
    for j in range(n):
        cs = slice(c * j, c * (j + 1))
        qb = q_ref[0, cs, :]
        kb = k_ref[0, cs, :]
        vb = v_ref[0, cs, :]
        zero_k = jnp.zeros_like(kb)
        kcat = jnp.concatenate([jnp.where(lane < half, kb, zero_k), jnp.where(lane >= half, kb, zero_k)], axis=0)
        scores = lax.dot_general(qb, kcat, (((1,), (1,)), ((), ())), preferred_element_type=F32)
        p = (scores * mcat).astype(BF16)
        zero_v = jnp.zeros_like(vb)
        vcat = jnp.concatenate([jnp.where(vlane < LANES, vb, zero_v), jnp.where(vlane >= LANES, vb, zero_v)], axis=0)
        qf = qb.astype(F32)
        qq = jnp.concatenate([qf * qdec_f, qf * qdec_b], axis=1).astype(BF16)
        y = _dot(p, vcat) + _dot(qq, s_ref[j])
        outs = []
        for hh in range(2):
            yh = y[:, LANES * hh:LANES * (hh + 1)]
            mu = jnp.mean(yh, axis=-1, keepdims=True)
            yc = yh - mu
            var = jnp.mean(yc * yc, axis=-1, keepdims=True)
            outs.append(yc * lax.rsqrt(var + EPS))
        yn = jnp.concatenate(outs, axis=1) * gn
        o_ref[0, cs, :] = (yn * sg_ref[0, cs, :].astype(F32)).astype(BF16)


def _retention(q, k, v, sg, states, decf, decb, gn_w):
    b, l, v_width = v.shape
    pairs = RET_HEADS // 2
    n = l // RET_CHUNK
    return pl.pallas_call(
        _ret_kernel,
        grid=(b, pairs),
        in_specs=[
            pl.BlockSpec((1, l, LANES), lambda i, p: (i, 0, p)),
            pl.BlockSpec((1, l, LANES), lambda i, p: (i, 0, p)),
            pl.BlockSpec((1, l, 2 * LANES), lambda i, p: (i, 0, p)),
            pl.BlockSpec((1, l, 2 * LANES), lambda i, p: (i, 0, p)),
            pl.BlockSpec((1, 1, 2 * LANES, 2 * LANES), lambda i, p: (i, p, 0, 0)),
            pl.BlockSpec((2, 1, LANES), lambda i, p: (p, 0, 0)),
            pl.BlockSpec((2, 1, LANES), lambda i, p: (p, 0, 0)),
            pl.BlockSpec((1, 2 * LANES), lambda i, p: (0, p)),
        ],
        out_specs=pl.BlockSpec((1, l, 2 * LANES), lambda i, p: (i, 0, p)),
        out_shape=jax.ShapeDtypeStruct((b, l, v_width), BF16),
        scratch_shapes=[pltpu.VMEM((n, 2 * LANES, 2 * LANES), F32), pltpu.VMEM((n, 2 * LANES, 2 * LANES), BF16)],
        compiler_params=_params("arbitrary", "arbitrary"),
        name="retention",
    )(q, k, v, sg, states, decf, decb, gn_w)


def _merge_kernel(yp_ref, yr_ref, ga_ref, gb_ref, x_ref, gm_ref, wpa_ref, wrb_ref, wo_ref, o_ref):
    a = _dot(yp_ref[0], wpa_ref[...])
    bb = _dot(yr_ref[0], wrb_ref[...])
    m = (ga_ref[0].astype(F32) * a + gb_ref[0].astype(F32) * bb).astype(BF16)
    o_ref[0] = x_ref[0] + gm_ref[0] * _dot(m, wo_ref[...])


def _merge(yp, yr, ga, gb, x, mod3, w_pa, w_rb, w_o):
    b, l, d = x.shape
    tm = TOKEN_TILE
    pw = yp.shape[2]
    const = lambda shape: pl.BlockSpec(shape, lambda i, t: (0, 0), pipeline_mode=pl.Buffered(1))
    return pl.pallas_call(
        _merge_kernel,
        grid=(b, l // tm),
        in_specs=[
            pl.BlockSpec((1, tm, pw), lambda i, t: (i, t, 0)),
            pl.BlockSpec((1, tm, yr.shape[2]), lambda i, t: (i, t, 0)),
            pl.BlockSpec((1, tm, d), lambda i, t: (i, t, 0)),
            pl.BlockSpec((1, tm, d), lambda i, t: (i, t, 0)),
            pl.BlockSpec((1, tm, d), lambda i, t: (i, t, 0)),
            pl.BlockSpec((1, 1, d), lambda i, t: (i, 0, 2)),
            const(w_pa.shape), const(w_rb.shape), const(w_o.shape),
        ],
        out_specs=pl.BlockSpec((1, tm, d), lambda i, t: (i, t, 0)),
        out_shape=jax.ShapeDtypeStruct((b, l, d), F32),
        compiler_params=_params("arbitrary", "arbitrary"),
        name="merge",
    )(yp, yr, ga, gb, x, mod3, w_pa, w_rb, w_o)


def _ffn_kernel(x_ref, sh_ref, sc_ref, g_ref, gain_ref, gfin_ref, w1_ref, w3_ref, w2_ref, o_ref, act_ref):
    x = x_ref[0]
    h = _rms_modulate(x, gain_ref[...], sh_ref[0], sc_ref[0]).astype(BF16)
    f = w1_ref.shape[1]
    for c0 in range(0, f, FFN_TILE):
        a = _dot(h, w1_ref[:, c0:c0 + FFN_TILE])
        bb = _dot(h, w3_ref[:, c0:c0 + FFN_TILE])
        act_ref[:, c0:c0 + FFN_TILE] = (_silu(a) * bb).astype(BF16)
    x2 = x + g_ref[0] * _dot(act_ref[...], w2_ref[...])
    o_ref[0] = x2 * lax.rsqrt(jnp.mean(x2 * x2, axis=-1, keepdims=True) + EPS) * gfin_ref[...]


def _ffn(x, mod3, gain, gfin, w1, w3, w2):
    b, l, d = x.shape
    f = w1.shape[1]
    tm = TOKEN_TILE
    const = lambda shape: pl.BlockSpec(shape, lambda i, t: (0, 0), pipeline_mode=pl.Buffered(1))
    return pl.pallas_call(
        _ffn_kernel,
        grid=(b, l // tm),
        in_specs=[
            pl.BlockSpec((1, tm, d), lambda i, t: (i, t, 0)),
            pl.BlockSpec((1, 1, d), lambda i, t: (i, 0, 3)),
            pl.BlockSpec((1, 1, d), lambda i, t: (i, 0, 4)),
            pl.BlockSpec((1, 1, d), lambda i, t: (i, 0, 5)),
            const((1, d)), const((1, d)),
            const(w1.shape), const(w3.shape), const(w2.shape),
        ],
        out_specs=pl.BlockSpec((1, tm, d), lambda i, t: (i, t, 0)),
        out_shape=jax.ShapeDtypeStruct((b, l, d), F32),
        scratch_shapes=[pltpu.VMEM((tm, f), BF16)],
        compiler_params=_params("arbitrary", "arbitrary"),
        name="ffn",
    )(x, mod3, mod3, mod3, gain, gfin, w1, w3, w2)


def _rope_tables(l, qk_dim, width):
    t = jnp.arange(l)
    row = (t // GRID_W).astype(F32)
    col = (t % GRID_W).astype(F32)
    n_freq = qk_dim // 4
    inv_freq = ROPE_BASE ** (-jnp.arange(n_freq, dtype=F32) / n_freq)
    ang = jnp.concatenate([row[:, None] * inv_freq, col[:, None] * inv_freq], axis=-1)
    cos, sin = jnp.cos(ang), jnp.sin(ang)
    reps = width // qk_dim
    return jnp.tile(jnp.concatenate([cos, cos], axis=-1), (1, reps)), jnp.tile(jnp.concatenate([-sin, sin], axis=-1), (1, reps))


def _pool_tables():
    t = np.arange(POOL_BLOCK)
    same_row = (t[:, None] // GRID_W) == (t[None, :] // GRID_W)
    off = t[None, :] - t[:, None]
    col = t % GRID_W
    acol, invc = [], []
    for w in POOL_WINDOWS:
        acol.append((same_row & (off >= -(w // 2)) & (off < w - w // 2)).astype(np.float32))
        cnt = np.minimum(col + w - w // 2, GRID_W) - np.maximum(col - w // 2, 0)
        invc.append(np.broadcast_to((1.0 / cnt.astype(np.float64)).astype(np.float32)[:, None], (POOL_BLOCK, LANES)))
    return jnp.asarray(np.stack(acol), dtype=BF16), jnp.asarray(np.stack(invc), dtype=F32)


def kernel(x, c, ctx, c_ctx, w_ada, b_ada, norm_mix, norm_ffn, w_in, w_pool, pool_scale,
           ret_decay_f, ret_decay_b, ret_gn_w, w_pa, w_rb, w_o, w_ff1, w_ff3, w_ff2, norm_final):
    assert w_ada.shape[0] == 1, "single-layer block"
    b, l, d = x.shape
    pool_width = w_pa.shape[1]
    v_width = w_rb.shape[1]
    qk_width = (w_in.shape[2] - pool_width - 2 * v_width - 2 * d) // 2
    qk_dim = qk_width // RET_HEADS
    k_scale = float(qk_dim) ** -0.5
    q_off = pool_width
    k_off = q_off + qk_width
    v_off = k_off + qk_width
    g_off = v_off + v_width
    ga_off = g_off + v_width
    gb_off = ga_off + d
    assert b < MOD_ROWS // 2 + 1 and qk_dim == LANES // 2 and v_width == RET_HEADS * LANES and qk_width == COL_TILE
    sections = (("raw", 0, q_off), ("q", q_off, k_off), ("k", k_off, v_off), ("raw", v_off, g_off),
                ("silu", g_off, ga_off), ("sigmoid", ga_off, gb_off), ("sigmoid", gb_off, gb_off + d))

    cc = jnp.zeros((MOD_ROWS, d), F32).at[:b].set(c).at[MOD_ROWS // 2].set(c_ctx)
    mod = _modulation(cc, w_ada[0], b_ada[0])
    mod3 = mod.reshape(MOD_ROWS, 1, 6 * d)

    w_in_b = w_in[0].astype(BF16)
    decf = jnp.broadcast_to(ret_decay_f[0][:, None, None], (RET_HEADS, 1, LANES))
    decb = jnp.broadcast_to(ret_decay_b[0][:, None, None], (RET_HEADS, 1, LANES))
    gain_mix = norm_mix[0].reshape(1, d)

    states = _ctx_states(ctx, mod3, gain_mix, w_in_b[:, k_off:g_off], decf, decb, k_scale)

    cos_t, sin_t = _rope_tables(l, qk_dim, qk_width)
    u, q, k, v, sg, ga, gb = _in_proj(x, mod3, gain_mix, w_in_b, cos_t, sin_t, sections, k_scale)

    acol, invc = _pool_tables()
    yp = _pool(u, acol, invc, w_pool[0].astype(BF16), pool_scale[0].reshape(1, pool_width))
    yr = _retention(q, k, v, sg, states, decf, decb, ret_gn_w[0].reshape(1, v_width))
    x1 = _merge(yp, yr, ga, gb, x, mod3, w_pa[0].astype(BF16), w_rb[0].astype(BF16), w_o[0].astype(BF16))
    return _ffn(x1, mod3, norm_ffn[0].reshape(1, d), norm_final.reshape(1, d),
                w_ff1[0].astype(BF16), w_ff3[0].astype(BF16), w_ff2[0].astype(BF16))
```

```python
import functools

import numpy as np
import jax
import jax.numpy as jnp
from jax import lax
from jax.experimental import pallas as pl
from jax.experimental.pallas import tpu as pltpu

F32 = jnp.float32
BF16 = jnp.bfloat16

GRID_W = 64
POOL_WINDOWS = (2, 4, 8, 16)
RET_HEADS = 8
RET_CHUNK = 128
ROPE_BASE = 10000.0
EPS = 1e-6

LANES = 128
MOD_ROWS = 16
VMEM_LIMIT_BYTES = 56 * 1024 * 1024

TOKEN_TILE = 512
COL_TILE = 512
POOL_BLOCK = 256
FFN_TILE = 256


def _silu(t):
    return t * jax.nn.sigmoid(t)


def _log_sigmoid(t):
    return jnp.minimum(t, 0.0) - jnp.log1p(jnp.exp(-jnp.abs(t)))


def _rms_modulate(x, gain, shift, scale):
    y = x * lax.rsqrt(jnp.mean(x * x, axis=-1, keepdims=True) + EPS) * gain
    return y * (1.0 + scale) + shift


def _dot(a, b):
    return jnp.dot(a, b, preferred_element_type=F32)


def _params(*semantics):
    return pltpu.CompilerParams(dimension_semantics=semantics, vmem_limit_bytes=VMEM_LIMIT_BYTES)


def _mod_kernel(c_ref, w_ref, b_ref, o_ref):
    s = _silu(c_ref[...]).astype(BF16)
    o_ref[...] = _dot(s, w_ref[...].astype(BF16)) + b_ref[...]


def _modulation(cc, w_ada, b_ada):
    d, n = w_ada.shape
    tn = d
    return pl.pallas_call(
        _mod_kernel,
        grid=(n // tn,),
        in_specs=[
            pl.BlockSpec((MOD_ROWS, d), lambda j: (0, 0)),
            pl.BlockSpec((d, tn), lambda j: (0, j)),
            pl.BlockSpec((1, tn), lambda j: (0, j)),
        ],
        out_specs=pl.BlockSpec((MOD_ROWS, tn), lambda j: (0, j)),
        out_shape=jax.ShapeDtypeStruct((MOD_ROWS, n), F32),
        compiler_params=_params("arbitrary"),
        name="modulation",
    )(cc, w_ada, b_ada.reshape(1, n))


def _ctx_kernel(ctx_ref, sh_ref, sc_ref, gain_ref, wk_ref, wv0_ref, wv1_ref, decf_ref, decb_ref, o_ref, wkv_ref,
                *, k_scale):
    lc = ctx_ref.shape[1]
    qk_width = wk_ref.shape[1]

    @pl.when(pl.program_id(0) == 0)
    def _():
        for i, w_ref in enumerate((wk_ref, wv0_ref, wv1_ref)):
            wkv_ref[:, qk_width * i:qk_width * (i + 1)] = w_ref[...].astype(BF16)

    h = _rms_modulate(ctx_ref[0], gain_ref[...], sh_ref[0], sc_ref[0]).astype(BF16)
    kv = _dot(h, wkv_ref[...])
    pos = lax.broadcasted_iota(jnp.int32, (lc, LANES), 0).astype(F32)
    lane = lax.broadcasted_iota(jnp.int32, (1, LANES), 1)
    row = lax.broadcasted_iota(jnp.int32, (2 * LANES, 2 * LANES), 0)
    col = lax.broadcasted_iota(jnp.int32, (2 * LANES, 2 * LANES), 1)
    same_head = ((row % LANES) < LANES // 2) == (col < LANES)
    for p in range(RET_HEADS // 2):
        k = kv[:, LANES * p:LANES * (p + 1)] * k_scale
        v = kv[:, qk_width + 2 * LANES * p:qk_width + 2 * LANES * (p + 1)].astype(BF16)
        lgf = jnp.where(lane < LANES // 2, _log_sigmoid(decf_ref[2 * p]), _log_sigmoid(decf_ref[2 * p + 1]))
        lgb = jnp.where(lane < LANES // 2, _log_sigmoid(decb_ref[2 * p]), _log_sigmoid(decb_ref[2 * p + 1]))
        kf = k * jnp.exp(lgf * (lc - 1.0 - pos))
        kb = k * jnp.exp(lgb * pos)
        kk = jnp.concatenate([kf, kb], axis=1).astype(BF16)
        st = lax.dot_general(kk, v, (((0,), (0,)), ((), ())), preferred_element_type=F32)
        o_ref[0, p] = jnp.where(same_head, st, 0.0)


def _ctx_states(ctx, mod3, gain, w_in, k_blk, decf, decb, k_scale):
    b, lc, d = ctx.shape
    pairs = RET_HEADS // 2
    ctx_row = MOD_ROWS // 2
    wblock = lambda j: pl.BlockSpec((d, COL_TILE), lambda i: (0, k_blk + j), pipeline_mode=pl.Buffered(1))
    return pl.pallas_call(
        functools.partial(_ctx_kernel, k_scale=k_scale),
        grid=(b,),
        in_specs=[
            pl.BlockSpec((1, lc, d), lambda i: (i, 0, 0)),
            pl.BlockSpec((1, 1, d), lambda i: (ctx_row, 0, 0)),
            pl.BlockSpec((1, 1, d), lambda i: (ctx_row, 0, 1)),
            pl.BlockSpec((1, d), lambda i: (0, 0)),
            wblock(0), wblock(1), wblock(2),
            pl.BlockSpec((RET_HEADS, 1, LANES), lambda i: (0, 0, 0)),
            pl.BlockSpec((RET_HEADS, 1, LANES), lambda i: (0, 0, 0)),
        ],
        out_specs=pl.BlockSpec((1, pairs, 2 * LANES, 2 * LANES), lambda i: (i, 0, 0, 0)),
        out_shape=jax.ShapeDtypeStruct((b, pairs, 2 * LANES, 2 * LANES), F32),
        scratch_shapes=[pltpu.VMEM((d, 3 * COL_TILE), BF16)],
        compiler_params=_params("arbitrary"),
        name="ctx_states",
    )(ctx, mod3, mod3, gain, w_in, w_in, w_in, decf, decb)


def _rope(t, cos, sin_signed):
    lane = lax.broadcasted_iota(jnp.int32, (1, LANES), 1)
    first_half = (lane % (LANES // 2)) < LANES // 4
    parts = []
    for i in range(t.shape[1] // LANES):
        s = t[:, LANES * i:LANES * (i + 1)]
        ahead = pltpu.roll(s, LANES - LANES // 4, axis=1)
        behind = pltpu.roll(s, LANES // 4, axis=1)
        parts.append(s * cos + jnp.where(first_half, ahead, behind) * sin_signed)
    return jnp.concatenate(parts, axis=1)


def _inproj_kernel(x_ref, sh_ref, sc_ref, gain_ref, w_ref, cos_ref, sin_ref, gn_ref, *refs, sections, k_scale, n_cast):
    cast_in, out_refs, cast_out = refs[:n_cast], refs[n_cast:-n_cast], refs[-n_cast:]
    for src, dst in zip(cast_in, cast_out):
        dst[...] = src[...].astype(BF16)
    h = _rms_modulate(x_ref[0], gain_ref[...], sh_ref[0], sc_ref[0]).astype(BF16)
    for (kind, lo, hi), o_ref in zip(sections, out_refs):
        for c0 in range(lo, hi, COL_TILE):
            acc = _dot(h, w_ref[:, c0:c0 + COL_TILE].astype(BF16))
            if kind == "q":
                acc = _rope(acc, cos_ref[...], sin_ref[...])
            elif kind == "k":
                acc = _rope(acc * k_scale, cos_ref[...], sin_ref[...])
            elif kind == "silu_gain":
                acc = _silu(acc) * gn_ref[:, c0 - lo:c0 - lo + COL_TILE]
            elif kind == "sigmoid":
                acc = jax.nn.sigmoid(acc)
            o_ref[0, :, c0 - lo:c0 - lo + COL_TILE] = acc.astype(BF16)


def _in_proj(x, mod3, gain, w_in, cos_t, sin_t, gn_w, sections, k_scale, cast_weights):
    b, l, d = x.shape
    n = w_in.shape[1]
    tm = TOKEN_TILE
    steps = (l // tm) * b
    n_cast_blocks = steps // 2
    cast_block = lambda i, bi: ((i * b + bi) // 2, 0)
    cast_specs = [pl.BlockSpec((w.shape[0] // n_cast_blocks, w.shape[1]), cast_block) for w in cast_weights]
    for w in cast_weights:
        assert w.shape[0] % (n_cast_blocks * 16) == 0
    return pl.pallas_call(
        functools.partial(_inproj_kernel, sections=sections, k_scale=k_scale, n_cast=len(cast_weights)),
        grid=(l // tm, b),
        in_specs=[
            pl.BlockSpec((1, tm, d), lambda i, bi: (bi, i, 0)),
            pl.BlockSpec((1, 1, d), lambda i, bi: (bi, 0, 0)),
            pl.BlockSpec((1, 1, d), lambda i, bi: (bi, 0, 1)),
            pl.BlockSpec((1, d), lambda i, bi: (0, 0)),
            pl.BlockSpec((d, n), lambda i, bi: (0, 0), pipeline_mode=pl.Buffered(1)),
            pl.BlockSpec((tm, LANES), lambda i, bi: (i, 0)),
            pl.BlockSpec((tm, LANES), lambda i, bi: (i, 0)),
            pl.BlockSpec(gn_w.shape, lambda i, bi: (0, 0)),
        ] + cast_specs,
        out_specs=[pl.BlockSpec((1, tm, hi - lo), lambda i, bi: (bi, i, 0)) for _, lo, hi in sections] + cast_specs,
        out_shape=[jax.ShapeDtypeStruct((b, l, hi - lo), BF16) for _, lo, hi in sections]
        + [jax.ShapeDtypeStruct(w.shape, BF16) for w in cast_weights],
        compiler_params=_params("arbitrary", "arbitrary"),
        name="in_proj",
    )(x, mod3, mod3, gain, w_in, cos_t, sin_t, gn_w, *cast_weights)


def _pool_kernel(u_ref, acol_ref, invc_ref, wpool_ref, scale_ref, o_ref, cm_ref, d_ref):
    l = u_ref.shape[1]
    rows = l // GRID_W
    for g, w in enumerate(POOL_WINDOWS):
        gs = slice(LANES * g, LANES * (g + 1))
        for t0 in range(0, l, POOL_BLOCK):
            cs = _dot(acol_ref[g], u_ref[0, t0:t0 + POOL_BLOCK, gs])
            cm_ref[t0:t0 + POOL_BLOCK, :] = cs * invc_ref[g]
        for r in range(rows):
            lo = max(r - w // 2, 0)
            hi = min(r + w - w // 2, rows)
            acc = cm_ref[GRID_W * lo:GRID_W * (lo + 1), :]
            for rr in range(lo + 1, hi):
                acc = acc + cm_ref[GRID_W * rr:GRID_W * (rr + 1), :]
            m = acc / float(hi - lo)
            rs = slice(GRID_W * r, GRID_W * (r + 1))
            d_ref[rs, :] = (m - u_ref[0, rs, gs].astype(F32)).astype(BF16)
        y = _dot(d_ref[...], wpool_ref[g]) * scale_ref[:, gs]
        o_ref[0, :, gs] = y.astype(BF16)


def _pool(u, acol, invc, w_pool, pool_scale):
    b, l, width = u.shape
    gd = w_pool.shape[1]
    return pl.pallas_call(
        _pool_kernel,
        grid=(b,),
        in_specs=[
            pl.BlockSpec((1, l, width), lambda i: (i, 0, 0)),
            pl.BlockSpec(acol.shape, lambda i: (0, 0, 0)),
            pl.BlockSpec(invc.shape, lambda i: (0, 0, 0)),
            pl.BlockSpec(w_pool.shape, lambda i: (0, 0, 0)),
            pl.BlockSpec((1, width), lambda i: (0, 0)),
        ],
        out_specs=pl.BlockSpec((1, l, width), lambda i: (i, 0, 0)),
        out_shape=jax.ShapeDtypeStruct((b, l, width), BF16),
        scratch_shapes=[pltpu.VMEM((l, gd), F32), pltpu.VMEM((l, gd), BF16)],
        compiler_params=_params("arbitrary"),
        name="pool",
    )(u, acol, invc, w_pool, pool_scale)


def _ret_kernel(q_ref, k_ref, v_ref, st_ref, decf_ref, decb_ref, o_ref, a_ref, rhs_ref, lhs_ref):
    c = RET_CHUNK
    l = q_ref.shape[1]
    n = l // c
    half = LANES // 2
    lane = lax.broadcasted_iota(jnp.int32, (1, LANES), 1)
    lgf0, lgf1 = _log_sigmoid(decf_ref[0]), _log_sigmoid(decf_ref[1])
    lgb0, lgb1 = _log_sigmoid(decb_ref[0]), _log_sigmoid(decb_ref[1])
    lgf = jnp.where(lane < half, lgf0, lgf1)
    lgb = jnp.where(lane < half, lgb0, lgb1)
    idx = lax.broadcasted_iota(jnp.int32, (c, LANES), 0).astype(F32)
    kdec_f = jnp.exp(lgf * (c - 1.0 - idx))
    kdec_b = jnp.exp(lgb * idx)
    qdec_f = jnp.exp(lgf * (idx + 1.0))
    qdec_b = jnp.exp(lgb * (c - idx))
    diff = idx - lax.broadcasted_iota(jnp.int32, (c, c), 1).astype(F32)

    def intra_mask(lf, lb):
        return jnp.where(diff >= 0.0, jnp.exp(lf * jnp.maximum(diff, 0.0)), jnp.exp(lb * jnp.maximum(-diff, 0.0)))

    mcat = jnp.concatenate([intra_mask(lgf0, lgb0), intra_mask(lgf1, lgb1)], axis=1)

    @pl.when((pl.program_id(0) == 0) & (pl.program_id(1) == 0))
    def _():
        rhs_ref[...] = jnp.zeros(rhs_ref.shape, BF16)

    head_rows = (slice(0, half), slice(half, LANES))
    head_cols = (slice(0, LANES), slice(LANES, 2 * LANES))

    for j in range(n):
        cs = slice(c * j, c * (j + 1))
        kj = k_ref[0, cs, :].astype(F32)
        kk = jnp.concatenate([kj * kdec_f, kj * kdec_b], axis=1).astype(BF16)
        a_ref[j] = lax.dot_general(kk, v_ref[0, cs, :], (((0,), (0,)), ((), ())), preferred_element_type=F32)
        for hh in range(2):
            rhs_ref[j, c * hh:c * (hh + 1), head_cols[hh]] = v_ref[0, cs, head_cols[hh]]

    for hh, (lf, lb) in enumerate(((lgf0, lgb0), (lgf1, lgb1))):
        rows, cols = head_rows[hh], head_cols[hh]
        sf = st_ref[0, 0, rows, cols]
        cdec = jnp.exp(lf * c)
        for j in range(n):
            rhs_ref[j, 2 * c + half * hh:2 * c + half * (hh + 1), cols] = sf.astype(BF16)
            sf = sf * cdec + a_ref[j, rows, cols]
        sb = st_ref[0, 0, LANES + half * hh:LANES + half * (hh + 1), cols]
        cdec = jnp.exp(lb * c)
        for j in reversed(range(n)):
            rhs_ref[j, 3 * c + half * hh:3 * c + half * (hh + 1), cols] = sb.astype(BF16)
            sb = sb * cdec + a_ref[j, LANES + half * hh:LANES + half * (hh + 1), cols]

    for j in range(n):
        cs = slice(c * j, c * (j + 1))
        qb = q_ref[0, cs, :]
        kb = k_ref[0, cs, :]
        zero_k = jnp.zeros_like(kb)
        kcat = jnp.concatenate([jnp.where(lane < half, kb, zero_k), jnp.where(lane >= half, kb, zero_k)], axis=0)
        scores = lax.dot_general(qb, kcat, (((1,), (1,)), ((), ())), preferred_element_type=F32)
        qf = qb.astype(F32)
        lhs_ref[j] = jnp.concatenate([scores * mcat, qf * qdec_f, qf * qdec_b], axis=1).astype(BF16)

    for j in range(n):
        o_ref[0, c * j:c * (j + 1), :] = _dot(lhs_ref[j], rhs_ref[j]).astype(BF16)


def _retention(q, k, v, states, decf, decb):
    b, l, v_width = v.shape
    pairs = RET_HEADS // 2
    n = l // RET_CHUNK
    return pl.pallas_call(
        _ret_kernel,
        grid=(b, pairs),
        in_specs=[
            pl.BlockSpec((1, l, LANES), lambda i, p: (i, 0, p)),
            pl.BlockSpec((1, l, LANES), lambda i, p: (i, 0, p)),
            pl.BlockSpec((1, l, 2 * LANES), lambda i, p: (i, 0, p)),
            pl.BlockSpec((1, 1, 2 * LANES, 2 * LANES), lambda i, p: (i, p, 0, 0)),
            pl.BlockSpec((2, 1, LANES), lambda i, p: (p, 0, 0)),
            pl.BlockSpec((2, 1, LANES), lambda i, p: (p, 0, 0)),
        ],
        out_specs=pl.BlockSpec((1, l, 2 * LANES), lambda i, p: (i, 0, p)),
        out_shape=jax.ShapeDtypeStruct((b, l, v_width), BF16),
        scratch_shapes=[pltpu.VMEM((n, 2 * LANES, 2 * LANES), F32),
                        pltpu.VMEM((n, 4 * RET_CHUNK, 2 * LANES), BF16),
                        pltpu.VMEM((n, RET_CHUNK, 4 * RET_CHUNK), BF16)],
        compiler_params=_params("arbitrary", "arbitrary"),
        name="retention",
    )(q, k, v, states, decf, decb)


def _merge_kernel(yp_ref, yr_ref, sg_ref, ga_ref, gb_ref, x_ref, gm_ref, wpa_ref, wrb_ref, wo_ref, o_ref):
    a = _dot(yp_ref[0], wpa_ref[...])
    parts = []
    for hh in range(RET_HEADS):
        hs = slice(LANES * hh, LANES * (hh + 1))
        yh = yr_ref[0, :, hs].astype(F32)
        yc = yh - jnp.mean(yh, axis=-1, keepdims=True)
        var = jnp.mean(yc * yc, axis=-1, keepdims=True)
        parts.append((yc * lax.rsqrt(var + EPS)).astype(BF16) * sg_ref[0, :, hs])
    bb = _dot(jnp.concatenate(parts, axis=1), wrb_ref[...])
    m = (ga_ref[0].astype(F32) * a + gb_ref[0].astype(F32) * bb).astype(BF16)
    o_ref[0] = x_ref[0] + gm_ref[0] * _dot(m, wo_ref[...])


def _merge(yp, yr, sg, ga, gb, x, mod3, w_pa, w_rb, w_o):
    b, l, d = x.shape
    tm = TOKEN_TILE
    pw = yp.shape[2]
    const = lambda shape: pl.BlockSpec(shape, lambda i, t: (0, 0), pipeline_mode=pl.Buffered(1))
    return pl.pallas_call(
        _merge_kernel,
        grid=(b, l // tm),
        in_specs=[
            pl.BlockSpec((1, tm, pw), lambda i, t: (i, t, 0)),
            pl.BlockSpec((1, tm, yr.shape[2]), lambda i, t: (i, t, 0)),
            pl.BlockSpec((1, tm, sg.shape[2]), lambda i, t: (i, t, 0)),
            pl.BlockSpec((1, tm, d), lambda i, t: (i, t, 0)),
            pl.BlockSpec((1, tm, d), lambda i, t: (i, t, 0)),
            pl.BlockSpec((1, tm, d), lambda i, t: (i, t, 0)),
            pl.BlockSpec((1, 1, d), lambda i, t: (i, 0, 2)),
            const(w_pa.shape), const(w_rb.shape), const(w_o.shape),
        ],
        out_specs=pl.BlockSpec((1, tm, d), lambda i, t: (i, t, 0)),
        out_shape=jax.ShapeDtypeStruct((b, l, d), F32),
        compiler_params=_params("arbitrary", "arbitrary"),
        name="merge",
    )(yp, yr, sg, ga, gb, x, mod3, w_pa, w_rb, w_o)


def _ffn_kernel(x_ref, sh_ref, sc_ref, g_ref, gain_ref, gfin_ref, w1_ref, w3_ref, w2_ref, o_ref, act_ref):
    x = x_ref[0]
    h = _rms_modulate(x, gain_ref[...], sh_ref[0], sc_ref[0]).astype(BF16)
    f = w1_ref.shape[1]
    for c0 in range(0, f, FFN_TILE):
        a = _dot(h, w1_ref[:, c0:c0 + FFN_TILE])
        bb = _dot(h, w3_ref[:, c0:c0 + FFN_TILE])
        act_ref[:, c0:c0 + FFN_TILE] = (_silu(a) * bb).astype(BF16)
    x2 = x + g_ref[0] * _dot(act_ref[...], w2_ref[...])
    o_ref[0] = x2 * lax.rsqrt(jnp.mean(x2 * x2, axis=-1, keepdims=True) + EPS) * gfin_ref[...]


def _ffn(x, mod3, gain, gfin, w1, w3, w2):
    b, l, d = x.shape
    f = w1.shape[1]
    tm = TOKEN_TILE
    const = lambda shape: pl.BlockSpec(shape, lambda i, t: (0, 0), pipeline_mode=pl.Buffered(1))
    return pl.pallas_call(
        _ffn_kernel,
        grid=(b, l // tm),
        in_specs=[
            pl.BlockSpec((1, tm, d), lambda i, t: (i, t, 0)),
            pl.BlockSpec((1, 1, d), lambda i, t: (i, 0, 3)),
            pl.BlockSpec((1, 1, d), lambda i, t: (i, 0, 4)),
            pl.BlockSpec((1, 1, d), lambda i, t: (i, 0, 5)),
            const((1, d)), const((1, d)),
            const(w1.shape), const(w3.shape), const(w2.shape),
        ],
        out_specs=pl.BlockSpec((1, tm, d), lambda i, t: (i, t, 0)),
        out_shape=jax.ShapeDtypeStruct((b, l, d), F32),
        scratch_shapes=[pltpu.VMEM((tm, f), BF16)],
        compiler_params=_params("arbitrary", "arbitrary"),
        name="ffn",
    )(x, mod3, mod3, mod3, gain, gfin, w1, w3, w2)


def _rope_tables(l, qk_dim):
    t = np.arange(l)
    n_freq = qk_dim // 4
    inv_freq = ROPE_BASE ** (-np.arange(n_freq, dtype=np.float64) / n_freq)
    ang = np.concatenate([(t // GRID_W)[:, None] * inv_freq, (t % GRID_W)[:, None] * inv_freq], axis=-1)
    cos, sin = np.cos(ang), np.sin(ang)
    reps = LANES // qk_dim
    cos_t = np.tile(np.concatenate([cos, cos], axis=-1), (1, reps))
    sin_t = np.tile(np.concatenate([-sin, sin], axis=-1), (1, reps))
    return jnp.asarray(cos_t, dtype=F32), jnp.asarray(sin_t, dtype=F32)


def _pool_tables():
    t = np.arange(POOL_BLOCK)
    same_row = (t[:, None] // GRID_W) == (t[None, :] // GRID_W)
    off = t[None, :] - t[:, None]
    col = t % GRID_W
    acol, invc = [], []
    for w in POOL_WINDOWS:
        acol.append((same_row & (off >= -(w // 2)) & (off < w - w // 2)).astype(np.float32))
        cnt = np.minimum(col + w - w // 2, GRID_W) - np.maximum(col - w // 2, 0)
        invc.append(np.broadcast_to((1.0 / cnt.astype(np.float64)).astype(np.float32)[:, None], (POOL_BLOCK, LANES)))
    return jnp.asarray(np.stack(acol), dtype=BF16), jnp.asarray(np.stack(invc), dtype=F32)


def kernel(x, c, ctx, c_ctx, w_ada, b_ada, norm_mix, norm_ffn, w_in, w_pool, pool_scale,
           ret_decay_f, ret_decay_b, ret_gn_w, w_pa, w_rb, w_o, w_ff1, w_ff3, w_ff2, norm_final):
    assert w_ada.shape[0] == 1, "single-layer block"
    b, l, d = x.shape
    pool_width = w_pa.shape[1]
    v_width = w_rb.shape[1]
    qk_width = (w_in.shape[2] - pool_width - 2 * v_width - 2 * d) // 2
    qk_dim = qk_width // RET_HEADS
    k_scale = float(qk_dim) ** -0.5
    q_off = pool_width
    k_off = q_off + qk_width
    v_off = k_off + qk_width
    g_off = v_off + v_width
    ga_off = g_off + v_width
    gb_off = ga_off + d
    assert b < MOD_ROWS // 2 + 1 and qk_dim == LANES // 2 and v_width == RET_HEADS * LANES and qk_width == COL_TILE
    assert k_off % COL_TILE == 0 and g_off - k_off == 3 * COL_TILE
    sections = (("raw", 0, q_off), ("q", q_off, k_off), ("k", k_off, v_off), ("raw", v_off, g_off),
                ("silu_gain", g_off, ga_off), ("sigmoid", ga_off, gb_off), ("sigmoid", gb_off, gb_off + d))

    cc = jnp.zeros((MOD_ROWS, d), F32).at[:b].set(c).at[MOD_ROWS // 2].set(c_ctx)
    mod = _modulation(cc, w_ada[0], b_ada[0])
    mod3 = mod.reshape(MOD_ROWS, 1, 6 * d)

    decf = jnp.broadcast_to(ret_decay_f[0][:, None, None], (RET_HEADS, 1, LANES))
    decb = jnp.broadcast_to(ret_decay_b[0][:, None, None], (RET_HEADS, 1, LANES))
    gain_mix = norm_mix[0].reshape(1, d)

    states = _ctx_states(ctx, mod3, gain_mix, w_in[0], k_off // COL_TILE, decf, decb, k_scale)

    cos_t, sin_t = _rope_tables(l, qk_dim)
    (u, q, k, v, sg, ga, gb, w_pa_b, w_rb_b, w_o_b, w1_b, w3_b, w2_b) = _in_proj(
        x, mod3, gain_mix, w_in[0], cos_t, sin_t, ret_gn_w[0].reshape(1, v_width), sections, k_scale,
        (w_pa[0], w_rb[0], w_o[0], w_ff1[0], w_ff3[0], w_ff2[0]))

    acol, invc = _pool_tables()
    yp = _pool(u, acol, invc, w_pool[0].astype(BF16), pool_scale[0].reshape(1, pool_width))
    yr = _retention(q, k, v, states, decf, decb)
    x1 = _merge(yp, yr, sg, ga, gb, x, mod3, w_pa_b, w_rb_b, w_o_b)
    return _ffn(x1, mod3, norm_ffn[0].reshape(1, d), norm_final.reshape(1, d), w1_b, w3_b, w2_b)
```

```python
import functools

import numpy as np
import jax
import jax.numpy as jnp
from jax import lax
from jax.experimental import pallas as pl
from jax.experimental.pallas import tpu as pltpu

F32 = jnp.float32
BF16 = jnp.bfloat16

GRID_W = 64
POOL_WINDOWS = (2, 4, 8, 16)
RET_HEADS = 8
RET_CHUNK = 128
ROPE_BASE = 10000.0
EPS = 1e-6

LANES = 128
MOD_ROWS = 16
VMEM_LIMIT_BYTES = 56 * 1024 * 1024

TOKEN_TILE = 512
COL_TILE = 512
POOL_BLOCK = 256
FFN_TILE = 256


def _silu(t):
    return t * jax.nn.sigmoid(t)


def _log_sigmoid(t):
    return jnp.minimum(t, 0.0) - jnp.log1p(jnp.exp(-jnp.abs(t)))


def _rms_modulate(x, gain, shift, scale):
    y = x * lax.rsqrt(jnp.mean(x * x, axis=-1, keepdims=True) + EPS) * gain
    return y * (1.0 + scale) + shift


def _dot(a, b):
    return jnp.dot(a, b, preferred_element_type=F32)


def _params(*semantics):
    return pltpu.CompilerParams(dimension_semantics=semantics, vmem_limit_bytes=VMEM_LIMIT_BYTES)


def _mod_kernel(c_ref, w_ref, b_ref, o_ref):
    s = _silu(c_ref[...]).astype(BF16)
    o_ref[...] = _dot(s, w_ref[...].astype(BF16)) + b_ref[...]


def _modulation(cc, w_ada, b_ada):
    d, n = w_ada.shape
    tn = d
    return pl.pallas_call(
        _mod_kernel,
        grid=(n // tn,),
        in_specs=[
            pl.BlockSpec((MOD_ROWS, d), lambda j: (0, 0)),
            pl.BlockSpec((d, tn), lambda j: (0, j)),
            pl.BlockSpec((1, tn), lambda j: (0, j)),
        ],
        out_specs=pl.BlockSpec((MOD_ROWS, tn), lambda j: (0, j)),
        out_shape=jax.ShapeDtypeStruct((MOD_ROWS, n), F32),
        compiler_params=_params("arbitrary"),
        name="modulation",
    )(cc, w_ada, b_ada.reshape(1, n))


def _ctx_kernel(ctx_ref, sh_ref, sc_ref, gain_ref, wk_ref, wv0_ref, wv1_ref, decf_ref, decb_ref, o_ref, wkv_ref,
                *, k_scale):
    lc = ctx_ref.shape[1]
    qk_width = wk_ref.shape[1]

    @pl.when(pl.program_id(0) == 0)
    def _():
        for i, w_ref in enumerate((wk_ref, wv0_ref, wv1_ref)):
            wkv_ref[:, qk_width * i:qk_width * (i + 1)] = w_ref[...].astype(BF16)

    h = _rms_modulate(ctx_ref[0], gain_ref[...], sh_ref[0], sc_ref[0]).astype(BF16)
    kv = _dot(h, wkv_ref[...])
    pos = lax.broadcasted_iota(jnp.int32, (lc, LANES), 0).astype(F32)
    lane = lax.broadcasted_iota(jnp.int32, (1, LANES), 1)
    row = lax.broadcasted_iota(jnp.int32, (2 * LANES, 2 * LANES), 0)
    col = lax.broadcasted_iota(jnp.int32, (2 * LANES, 2 * LANES), 1)
    same_head = ((row % LANES) < LANES // 2) == (col < LANES)
    for p in range(RET_HEADS // 2):
        k = kv[:, LANES * p:LANES * (p + 1)] * k_scale
        v = kv[:, qk_width + 2 * LANES * p:qk_width + 2 * LANES * (p + 1)].astype(BF16)
        lgf = jnp.where(lane < LANES // 2, _log_sigmoid(decf_ref[2 * p]), _log_sigmoid(decf_ref[2 * p + 1]))
        lgb = jnp.where(lane < LANES // 2, _log_sigmoid(decb_ref[2 * p]), _log_sigmoid(decb_ref[2 * p + 1]))
        kf = k * jnp.exp(lgf * (lc - 1.0 - pos))
        kb = k * jnp.exp(lgb * pos)
        kk = jnp.concatenate([kf, kb], axis=1).astype(BF16)
        st = lax.dot_general(kk, v, (((0,), (0,)), ((), ())), preferred_element_type=F32)
        o_ref[0, p] = jnp.where(same_head, st, 0.0)


def _ctx_states(ctx, mod3, gain, w_in, k_blk, decf, decb, k_scale):
    b, lc, d = ctx.shape
    pairs = RET_HEADS // 2
    ctx_row = MOD_ROWS // 2
    wblock = lambda j: pl.BlockSpec((d, COL_TILE), lambda i: (0, k_blk + j), pipeline_mode=pl.Buffered(1))
    return pl.pallas_call(
        functools.partial(_ctx_kernel, k_scale=k_scale),
        grid=(b,),
        in_specs=[
            pl.BlockSpec((1, lc, d), lambda i: (i, 0, 0)),
            pl.BlockSpec((1, 1, d), lambda i: (ctx_row, 0, 0)),
            pl.BlockSpec((1, 1, d), lambda i: (ctx_row, 0, 1)),
            pl.BlockSpec((1, d), lambda i: (0, 0)),
            wblock(0), wblock(1), wblock(2),
            pl.BlockSpec((RET_HEADS, 1, LANES), lambda i: (0, 0, 0)),
            pl.BlockSpec((RET_HEADS, 1, LANES), lambda i: (0, 0, 0)),
        ],
        out_specs=pl.BlockSpec((1, pairs, 2 * LANES, 2 * LANES), lambda i: (i, 0, 0, 0)),
        out_shape=jax.ShapeDtypeStruct((b, pairs, 2 * LANES, 2 * LANES), F32),
        scratch_shapes=[pltpu.VMEM((d, 3 * COL_TILE), BF16)],
        compiler_params=_params("arbitrary"),
        name="ctx_states",
    )(ctx, mod3, mod3, gain, w_in, w_in, w_in, decf, decb)


def _rope(t, cos, sin_signed):
    lane = lax.broadcasted_iota(jnp.int32, (1, LANES), 1)
    first_half = (lane % (LANES // 2)) < LANES // 4
    parts = []
    for i in range(t.shape[1] // LANES):
        s = t[:, LANES * i:LANES * (i + 1)]
        ahead = pltpu.roll(s, LANES - LANES // 4, axis=1)
        behind = pltpu.roll(s, LANES // 4, axis=1)
        parts.append(s * cos + jnp.where(first_half, ahead, behind) * sin_signed)
    return jnp.concatenate(parts, axis=1)


def _inproj_kernel(x_ref, sh_ref, sc_ref, gain_ref, w_ref, cos_ref, sin_ref, gn_ref, *refs, sections, k_scale, n_cast):
    cast_in, out_refs, cast_out = refs[:n_cast], refs[n_cast:-n_cast], refs[-n_cast:]
    for src, dst in zip(cast_in, cast_out):
        dst[...] = src[...].astype(BF16)
    h = _rms_modulate(x_ref[0], gain_ref[...], sh_ref[0], sc_ref[0]).astype(BF16)
    for (kind, lo, hi), o_ref in zip(sections, out_refs):
        for c0 in range(lo, hi, COL_TILE):
            acc = _dot(h, w_ref[:, c0:c0 + COL_TILE].astype(BF16))
            if kind == "q":
                acc = _rope(acc, cos_ref[...], sin_ref[...])
            elif kind == "k":
                acc = _rope(acc * k_scale, cos_ref[...], sin_ref[...])
            elif kind == "silu_gain":
                acc = _silu(acc) * gn_ref[:, c0 - lo:c0 - lo + COL_TILE]
            elif kind == "sigmoid":
                acc = jax.nn.sigmoid(acc)
            o_ref[0, :, c0 - lo:c0 - lo + COL_TILE] = acc.astype(BF16)


def _in_proj(x, mod3, gain, w_in, cos_t, sin_t, gn_w, sections, k_scale, cast_weights):
    b, l, d = x.shape
    n = w_in.shape[1]
    tm = TOKEN_TILE
    steps = (l // tm) * b
    n_cast_blocks = steps // 2
    cast_block = lambda i, bi: ((i * b + bi) // 2, 0)
    cast_specs = [pl.BlockSpec((w.shape[0] // n_cast_blocks, w.shape[1]), cast_block) for w in cast_weights]
    for w in cast_weights:
        assert w.shape[0] % (n_cast_blocks * 16) == 0
    return pl.pallas_call(
        functools.partial(_inproj_kernel, sections=sections, k_scale=k_scale, n_cast=len(cast_weights)),
        grid=(l // tm, b),
        in_specs=[
            pl.BlockSpec((1, tm, d), lambda i, bi: (bi, i, 0)),
            pl.BlockSpec((1, 1, d), lambda i, bi: (bi, 0, 0)),
            pl.BlockSpec((1, 1, d), lambda i, bi: (bi, 0, 1)),
            pl.BlockSpec((1, d), lambda i, bi: (0, 0)),
            pl.BlockSpec((d, n), lambda i, bi: (0, 0), pipeline_mode=pl.Buffered(1)),
            pl.BlockSpec((tm, LANES), lambda i, bi: (i, 0)),
            pl.BlockSpec((tm, LANES), lambda i, bi: (i, 0)),
            pl.BlockSpec(gn_w.shape, lambda i, bi: (0, 0)),
        ] + cast_specs,
        out_specs=[pl.BlockSpec((1, tm, hi - lo), lambda i, bi: (bi, i, 0)) for _, lo, hi in sections] + cast_specs,
        out_shape=[jax.ShapeDtypeStruct((b, l, hi - lo), BF16) for _, lo, hi in sections]
        + [jax.ShapeDtypeStruct(w.shape, BF16) for w in cast_weights],
        compiler_params=_params("arbitrary", "arbitrary"),
        name="in_proj",
    )(x, mod3, mod3, gain, w_in, cos_t, sin_t, gn_w, *cast_weights)


def _pool_kernel(u_ref, acol_ref, invc_ref, wpool_ref, scale_ref, o_ref, cm_ref, d_ref):
    l = u_ref.shape[1]
    rows = l // GRID_W
    for g, w in enumerate(POOL_WINDOWS):
        gs = slice(LANES * g, LANES * (g + 1))
        for t0 in range(0, l, POOL_BLOCK):
            cs = _dot(acol_ref[g], u_ref[0, t0:t0 + POOL_BLOCK, gs])
            cm_ref[t0:t0 + POOL_BLOCK, :] = cs * invc_ref[g]
        for r in range(rows):
            lo = max(r - w // 2, 0)
            hi = min(r + w - w // 2, rows)
            acc = cm_ref[GRID_W * lo:GRID_W * (lo + 1), :]
            for rr in range(lo + 1, hi):
                acc = acc + cm_ref[GRID_W * rr:GRID_W * (rr + 1), :]
            m = acc / float(hi - lo)
            rs = slice(GRID_W * r, GRID_W * (r + 1))
            d_ref[rs, :] = (m - u_ref[0, rs, gs].astype(F32)).astype(BF16)
        y = _dot(d_ref[...], wpool_ref[g]) * scale_ref[:, gs]
        o_ref[0, :, gs] = y.astype(BF16)


def _pool(u, acol, invc, w_pool, pool_scale):
    b, l, width = u.shape
    gd = w_pool.shape[1]
    return pl.pallas_call(
        _pool_kernel,
        grid=(b,),
        in_specs=[
            pl.BlockSpec((1, l, width), lambda i: (i, 0, 0)),
            pl.BlockSpec(acol.shape, lambda i: (0, 0, 0)),
            pl.BlockSpec(invc.shape, lambda i: (0, 0, 0)),
            pl.BlockSpec(w_pool.shape, lambda i: (0, 0, 0)),
            pl.BlockSpec((1, width), lambda i: (0, 0)),
        ],
        out_specs=pl.BlockSpec((1, l, width), lambda i: (i, 0, 0)),
        out_shape=jax.ShapeDtypeStruct((b, l, width), BF16),
        scratch_shapes=[pltpu.VMEM((l, gd), F32), pltpu.VMEM((l, gd), BF16)],
        compiler_params=_params("arbitrary"),
        name="pool",
    )(u, acol, invc, w_pool, pool_scale)


def _ret_kernel(q_ref, k_ref, v_ref, st_ref, decf_ref, decb_ref, o_ref, a_ref, rhs_ref, lhs_ref):
    c = RET_CHUNK
    l = q_ref.shape[1]
    n = l // c
    half = LANES // 2
    lane = lax.broadcasted_iota(jnp.int32, (1, LANES), 1)
    lgf0, lgf1 = _log_sigmoid(decf_ref[0]), _log_sigmoid(decf_ref[1])
    lgb0, lgb1 = _log_sigmoid(decb_ref[0]), _log_sigmoid(decb_ref[1])
    lgf = jnp.where(lane < half, lgf0, lgf1)
    lgb = jnp.where(lane < half, lgb0, lgb1)
    idx = lax.broadcasted_iota(jnp.int32, (c, LANES), 0).astype(F32)
    kdec_f = jnp.exp(lgf * (c - 1.0 - idx))
    kdec_b = jnp.exp(lgb * idx)
    qdec_f = jnp.exp(lgf * (idx + 1.0))
    qdec_b = jnp.exp(lgb * (c - idx))
    diff = idx - lax.broadcasted_iota(jnp.int32, (c, c), 1).astype(F32)

    def intra_mask(lf, lb):
        return jnp.where(diff >= 0.0, jnp.exp(lf * jnp.maximum(diff, 0.0)), jnp.exp(lb * jnp.maximum(-diff, 0.0)))

    mcat = jnp.concatenate([intra_mask(lgf0, lgb0), intra_mask(lgf1, lgb1)], axis=1)

    @pl.when((pl.program_id(0) == 0) & (pl.program_id(1) == 0))
    def _():
        rhs_ref[...] = jnp.zeros(rhs_ref.shape, BF16)

    head_rows = (slice(0, half), slice(half, LANES))
    head_cols = (slice(0, LANES), slice(LANES, 2 * LANES))

    for j in range(n):
        cs = slice(c * j, c * (j + 1))
        kj = k_ref[0, cs, :].astype(F32)
        kk = jnp.concatenate([kj * kdec_f, kj * kdec_b], axis=1).astype(BF16)
        a_ref[j] = lax.dot_general(kk, v_ref[0, cs, :], (((0,), (0,)), ((), ())), preferred_element_type=F32)
        for hh in range(2):
            rhs_ref[j, c * hh:c * (hh + 1), head_cols[hh]] = v_ref[0, cs, head_cols[hh]]

    for hh, (lf, lb) in enumerate(((lgf0, lgb0), (lgf1, lgb1))):
        rows, cols = head_rows[hh], head_cols[hh]
        sf = st_ref[0, 0, rows, cols]
        cdec = jnp.exp(lf * c)
        for j in range(n):
            rhs_ref[j, 2 * c + half * hh:2 * c + half * (hh + 1), cols] = sf.astype(BF16)
            sf = sf * cdec + a_ref[j, rows, cols]
        sb = st_ref[0, 0, LANES + half * hh:LANES + half * (hh + 1), cols]
        cdec = jnp.exp(lb * c)
        for j in reversed(range(n)):
            rhs_ref[j, 3 * c + half * hh:3 * c + half * (hh + 1), cols] = sb.astype(BF16)
            sb = sb * cdec + a_ref[j, LANES + half * hh:LANES + half * (hh + 1), cols]

    for j in range(n):
        cs = slice(c * j, c * (j + 1))
        qb = q_ref[0, cs, :]
        kb = k_ref[0, cs, :]
        zero_k = jnp.zeros_like(kb)
        kcat = jnp.concatenate([jnp.where(lane < half, kb, zero_k), jnp.where(lane >= half, kb, zero_k)], axis=0)
        scores = lax.dot_general(qb, kcat, (((1,), (1,)), ((), ())), preferred_element_type=F32)
        qf = qb.astype(F32)
        lhs_ref[j] = jnp.concatenate([scores * mcat, qf * qdec_f, qf * qdec_b], axis=1).astype(BF16)

    for j in range(n):
        o_ref[0, c * j:c * (j + 1), :] = _dot(lhs_ref[j], rhs_ref[j]).astype(BF16)


def _retention(q, k, v, states, decf, decb):
    b, l, v_width = v.shape
    pairs = RET_HEADS // 2
    n = l // RET_CHUNK
    return pl.pallas_call(
        _ret_kernel,
        grid=(b, pairs),
        in_specs=[
            pl.BlockSpec((1, l, LANES), lambda i, p: (i, 0, p)),
            pl.BlockSpec((1, l, LANES), lambda i, p: (i, 0, p)),
            pl.BlockSpec((1, l, 2 * LANES), lambda i, p: (i, 0, p)),
            pl.BlockSpec((1, 1, 2 * LANES, 2 * LANES), lambda i, p: (i, p, 0, 0)),
            pl.BlockSpec((2, 1, LANES), lambda i, p: (p, 0, 0)),
            pl.BlockSpec((2, 1, LANES), lambda i, p: (p, 0, 0)),
        ],
        out_specs=pl.BlockSpec((1, l, 2 * LANES), lambda i, p: (i, 0, p)),
        out_shape=jax.ShapeDtypeStruct((b, l, v_width), BF16),
        scratch_shapes=[pltpu.VMEM((n, 2 * LANES, 2 * LANES), F32),
                        pltpu.VMEM((n, 4 * RET_CHUNK, 2 * LANES), BF16),
                        pltpu.VMEM((n, RET_CHUNK, 4 * RET_CHUNK), BF16)],
        compiler_params=_params("arbitrary", "arbitrary"),
        name="retention",
    )(q, k, v, states, decf, decb)


def _tail_kernel(yp_ref, yr_ref, sg_ref, ga_ref, gb_ref, x_ref, gm_ref, sh_ref, sc_ref, gf_ref, gain_ref, gfin_ref,
                 wpa_ref, wrb_ref, wo_ref, w1_ref, w3_ref, w2_ref, o_ref, act_ref):
    a = _dot(yp_ref[0], wpa_ref[...])
    parts = []
    for hh in range(RET_HEADS):
        hs = slice(LANES * hh, LANES * (hh + 1))
        yh = yr_ref[0, :, hs].astype(F32)
        yc = yh - jnp.mean(yh, axis=-1, keepdims=True)
        var = jnp.mean(yc * yc, axis=-1, keepdims=True)
        parts.append((yc * lax.rsqrt(var + EPS)).astype(BF16) * sg_ref[0, :, hs])
    bb = _dot(jnp.concatenate(parts, axis=1), wrb_ref[...])
    m = (ga_ref[0].astype(F32) * a + gb_ref[0].astype(F32) * bb).astype(BF16)
    x1 = x_ref[0] + gm_ref[0] * _dot(m, wo_ref[...])
    h = _rms_modulate(x1, gain_ref[...], sh_ref[0], sc_ref[0]).astype(BF16)
    f = w1_ref.shape[1]
    for c0 in range(0, f, FFN_TILE):
        u1 = _dot(h, w1_ref[:, c0:c0 + FFN_TILE])
        u3 = _dot(h, w3_ref[:, c0:c0 + FFN_TILE])
        act_ref[:, c0:c0 + FFN_TILE] = (_silu(u1) * u3).astype(BF16)
    x2 = x1 + gf_ref[0] * _dot(act_ref[...], w2_ref[...])
    o_ref[0] = x2 * lax.rsqrt(jnp.mean(x2 * x2, axis=-1, keepdims=True) + EPS) * gfin_ref[...]


def _tail(yp, yr, sg, ga, gb, x, mod3, gain, gfin, w_pa, w_rb, w_o, w1, w3, w2):
    b, l, d = x.shape
    f = w1.shape[1]
    tm = TOKEN_TILE
    const = lambda shape: pl.BlockSpec(shape, lambda i, t: (0, 0), pipeline_mode=pl.Buffered(1))
    tile = lambda width: pl.BlockSpec((1, tm, width), lambda i, t: (i, t, 0))
    mod_row = lambda section: pl.BlockSpec((1, 1, d), lambda i, t: (i, 0, section))
    return pl.pallas_call(
        _tail_kernel,
        grid=(b, l // tm),
        in_specs=[
            tile(yp.shape[2]), tile(yr.shape[2]), tile(sg.shape[2]), tile(d), tile(d), tile(d),
            mod_row(2), mod_row(3), mod_row(4), mod_row(5),
            const((1, d)), const((1, d)),
            const(w_pa.shape), const(w_rb.shape), const(w_o.shape), const(w1.shape), const(w3.shape), const(w2.shape),
        ],
        out_specs=tile(d),
        out_shape=jax.ShapeDtypeStruct((b, l, d), F32),
        scratch_shapes=[pltpu.VMEM((tm, f), BF16)],
        compiler_params=_params("arbitrary", "arbitrary"),
        name="tail",
    )(yp, yr, sg, ga, gb, x, mod3, mod3, mod3, mod3, gain, gfin, w_pa, w_rb, w_o, w1, w3, w2)


def _rope_tables(l, qk_dim):
    t = np.arange(l)
    n_freq = qk_dim // 4
    inv_freq = ROPE_BASE ** (-np.arange(n_freq, dtype=np.float64) / n_freq)
    ang = np.concatenate([(t // GRID_W)[:, None] * inv_freq, (t % GRID_W)[:, None] * inv_freq], axis=-1)
    cos, sin = np.cos(ang), np.sin(ang)
    reps = LANES // qk_dim
    cos_t = np.tile(np.concatenate([cos, cos], axis=-1), (1, reps))
    sin_t = np.tile(np.concatenate([-sin, sin], axis=-1), (1, reps))
    return jnp.asarray(cos_t, dtype=F32), jnp.asarray(sin_t, dtype=F32)


def _pool_tables():
    t = np.arange(POOL_BLOCK)
    same_row = (t[:, None] // GRID_W) == (t[None, :] // GRID_W)
    off = t[None, :] - t[:, None]
    col = t % GRID_W
    acol, invc = [], []
    for w in POOL_WINDOWS:
        acol.append((same_row & (off >= -(w // 2)) & (off < w - w // 2)).astype(np.float32))
        cnt = np.minimum(col + w - w // 2, GRID_W) - np.maximum(col - w // 2, 0)
        invc.append(np.broadcast_to((1.0 / cnt.astype(np.float64)).astype(np.float32)[:, None], (POOL_BLOCK, LANES)))
    return jnp.asarray(np.stack(acol), dtype=BF16), jnp.asarray(np.stack(invc), dtype=F32)


def kernel(x, c, ctx, c_ctx, w_ada, b_ada, norm_mix, norm_ffn, w_in, w_pool, pool_scale,
           ret_decay_f, ret_decay_b, ret_gn_w, w_pa, w_rb, w_o, w_ff1, w_ff3, w_ff2, norm_final):
    assert w_ada.shape[0] == 1, "single-layer block"
    b, l, d = x.shape
    pool_width = w_pa.shape[1]
    v_width = w_rb.shape[1]
    qk_width = (w_in.shape[2] - pool_width - 2 * v_width - 2 * d) // 2
    qk_dim = qk_width // RET_HEADS
    k_scale = float(qk_dim) ** -0.5
    q_off = pool_width
    k_off = q_off + qk_width
    v_off = k_off + qk_width
    g_off = v_off + v_width
    ga_off = g_off + v_width
    gb_off = ga_off + d
    assert b < MOD_ROWS // 2 + 1 and qk_dim == LANES // 2 and v_width == RET_HEADS * LANES and qk_width == COL_TILE
    assert k_off % COL_TILE == 0 and g_off - k_off == 3 * COL_TILE
    sections = (("raw", 0, q_off), ("q", q_off, k_off), ("k", k_off, v_off), ("raw", v_off, g_off),
                ("silu_gain", g_off, ga_off), ("sigmoid", ga_off, gb_off), ("sigmoid", gb_off, gb_off + d))

    cc = jnp.zeros((MOD_ROWS, d), F32).at[:b].set(c).at[MOD_ROWS // 2].set(c_ctx)
    mod = _modulation(cc, w_ada[0], b_ada[0])
    mod3 = mod.reshape(MOD_ROWS, 1, 6 * d)

    decf = jnp.broadcast_to(ret_decay_f[0][:, None, None], (RET_HEADS, 1, LANES))
    decb = jnp.broadcast_to(ret_decay_b[0][:, None, None], (RET_HEADS, 1, LANES))
    gain_mix = norm_mix[0].reshape(1, d)

    states = _ctx_states(ctx, mod3, gain_mix, w_in[0], k_off // COL_TILE, decf, decb, k_scale)

    cos_t, sin_t = _rope_tables(l, qk_dim)
    (u, q, k, v, sg, ga, gb, w_pa_b, w_rb_b, w_o_b, w1_b, w3_b, w2_b) = _in_proj(
        x, mod3, gain_mix, w_in[0], cos_t, sin_t, ret_gn_w[0].reshape(1, v_width), sections, k_scale,
        (w_pa[0], w_rb[0], w_o[0], w_ff1[0], w_ff3[0], w_ff2[0]))

    acol, invc = _pool_tables()
    yp = _pool(u, acol, invc, w_pool[0].astype(BF16), pool_scale[0].reshape(1, pool_width))
    yr = _retention(q, k, v, states, decf, decb)
    return _tail(yp, yr, sg, ga, gb, x, mod3, norm_ffn[0].reshape(1, d), norm_final.reshape(1, d),
                 w_pa_b, w_rb_b, w_o_b, w1_b, w3_b, w2_b)
```

```python
import functools

import numpy as np
import jax
import jax.numpy as jnp
from jax import lax
from jax.experimental import pallas as pl
from jax.experimental.pallas import tpu as pltpu

F32 = jnp.float32
BF16 = jnp.bfloat16

GRID_W = 64
POOL_WINDOWS = (2, 4, 8, 16)
RET_HEADS = 8
RET_CHUNK = 128
ROPE_BASE = 10000.0
EPS = 1e-6

LANES = 128
MOD_ROWS = 16
VMEM_LIMIT_BYTES = 56 * 1024 * 1024

TOKEN_TILE = 512
COL_TILE = 512
POOL_BLOCK = 256
FFN_TILE = 256


def _silu(t):
    return t * jax.nn.sigmoid(t)


def _log_sigmoid(t):
    return jnp.minimum(t, 0.0) - jnp.log1p(jnp.exp(-jnp.abs(t)))


def _rms_modulate(x, gain, shift, scale):
    y = x * lax.rsqrt(jnp.mean(x * x, axis=-1, keepdims=True) + EPS) * gain
    return y * (1.0 + scale) + shift


def _dot(a, b):
    return jnp.dot(a, b, preferred_element_type=F32)


def _params(*semantics):
    return pltpu.CompilerParams(dimension_semantics=semantics, vmem_limit_bytes=VMEM_LIMIT_BYTES)


def _mod_kernel(c_ref, w_ref, b_ref, o_ref):
    s = _silu(c_ref[...]).astype(BF16)
    o_ref[...] = _dot(s, w_ref[...].astype(BF16)) + b_ref[...]


def _modulation(cc, w_ada, b_ada):
    d, n = w_ada.shape
    tn = d
    return pl.pallas_call(
        _mod_kernel,
        grid=(n // tn,),
        in_specs=[
            pl.BlockSpec((MOD_ROWS, d), lambda j: (0, 0)),
            pl.BlockSpec((d, tn), lambda j: (0, j)),
            pl.BlockSpec((1, tn), lambda j: (0, j)),
        ],
        out_specs=pl.BlockSpec((MOD_ROWS, tn), lambda j: (0, j)),
        out_shape=jax.ShapeDtypeStruct((MOD_ROWS, n), F32),
        compiler_params=_params("arbitrary"),
        name="modulation",
    )(cc, w_ada, b_ada.reshape(1, n))


def _ctx_kernel(ctx_ref, sh_ref, sc_ref, gain_ref, wk_ref, wv0_ref, wv1_ref, decf_ref, decb_ref, wrow_ref,
                o_ref, wrow_out_ref, wkv_ref, *, k_scale):
    lc = ctx_ref.shape[1]
    qk_width = wk_ref.shape[1]
    wrow_out_ref[...] = wrow_ref[...].astype(BF16)

    @pl.when(pl.program_id(0) == 0)
    def _():
        for i, w_ref in enumerate((wk_ref, wv0_ref, wv1_ref)):
            wkv_ref[:, qk_width * i:qk_width * (i + 1)] = w_ref[...].astype(BF16)

    h = _rms_modulate(ctx_ref[0], gain_ref[...], sh_ref[0], sc_ref[0]).astype(BF16)
    kv = _dot(h, wkv_ref[...])
    pos = lax.broadcasted_iota(jnp.int32, (lc, LANES), 0).astype(F32)
    lane = lax.broadcasted_iota(jnp.int32, (1, LANES), 1)
    row = lax.broadcasted_iota(jnp.int32, (2 * LANES, 2 * LANES), 0)
    col = lax.broadcasted_iota(jnp.int32, (2 * LANES, 2 * LANES), 1)
    same_head = ((row % LANES) < LANES // 2) == (col < LANES)
    for p in range(RET_HEADS // 2):
        k = kv[:, LANES * p:LANES * (p + 1)] * k_scale
        v = kv[:, qk_width + 2 * LANES * p:qk_width + 2 * LANES * (p + 1)].astype(BF16)
        lgf = jnp.where(lane < LANES // 2, _log_sigmoid(decf_ref[2 * p]), _log_sigmoid(decf_ref[2 * p + 1]))
        lgb = jnp.where(lane < LANES // 2, _log_sigmoid(decb_ref[2 * p]), _log_sigmoid(decb_ref[2 * p + 1]))
        kf = k * jnp.exp(lgf * (lc - 1.0 - pos))
        kb = k * jnp.exp(lgb * pos)
        kk = jnp.concatenate([kf, kb], axis=1).astype(BF16)
        st = lax.dot_general(kk, v, (((0,), (0,)), ((), ())), preferred_element_type=F32)
        o_ref[0, p] = jnp.where(same_head, st, 0.0)


def _ctx_states(ctx, mod3, gain, w_in, k_blk, decf, decb, k_scale):
    b, lc, d = ctx.shape
    n = w_in.shape[1]
    pairs = RET_HEADS // 2
    ctx_row = MOD_ROWS // 2
    wblock = lambda j: pl.BlockSpec((d, COL_TILE), lambda i: (0, k_blk + j), pipeline_mode=pl.Buffered(1))
    assert d % (b * 16) == 0
    wrow = pl.BlockSpec((d // b, n), lambda i: (i, 0))
    return pl.pallas_call(
        functools.partial(_ctx_kernel, k_scale=k_scale),
        grid=(b,),
        in_specs=[
            pl.BlockSpec((1, lc, d), lambda i: (i, 0, 0)),
            pl.BlockSpec((1, 1, d), lambda i: (ctx_row, 0, 0)),
            pl.BlockSpec((1, 1, d), lambda i: (ctx_row, 0, 1)),
            pl.BlockSpec((1, d), lambda i: (0, 0)),
            wblock(0), wblock(1), wblock(2),
            pl.BlockSpec((RET_HEADS, 1, LANES), lambda i: (0, 0, 0)),
            pl.BlockSpec((RET_HEADS, 1, LANES), lambda i: (0, 0, 0)),
            wrow,
        ],
        out_specs=[pl.BlockSpec((1, pairs, 2 * LANES, 2 * LANES), lambda i: (i, 0, 0, 0)), wrow],
        out_shape=[jax.ShapeDtypeStruct((b, pairs, 2 * LANES, 2 * LANES), F32), jax.ShapeDtypeStruct((d, n), BF16)],
        scratch_shapes=[pltpu.VMEM((d, 3 * COL_TILE), BF16)],
        compiler_params=_params("arbitrary"),
        name="ctx_states",
    )(ctx, mod3, mod3, gain, w_in, w_in, w_in, decf, decb, w_in)


def _rope(t, cos, sin_signed):
    lane = lax.broadcasted_iota(jnp.int32, (1, LANES), 1)
    first_half = (lane % (LANES // 2)) < LANES // 4
    parts = []
    for i in range(t.shape[1] // LANES):
        s = t[:, LANES * i:LANES * (i + 1)]
        ahead = pltpu.roll(s, LANES - LANES // 4, axis=1)
        behind = pltpu.roll(s, LANES // 4, axis=1)
        parts.append(s * cos + jnp.where(first_half, ahead, behind) * sin_signed)
    return jnp.concatenate(parts, axis=1)


def _inproj_kernel(x0_ref, xn_ref, sh0_ref, sc0_ref, shn_ref, scn_ref, gain_ref, w_ref, cos_ref, sin_ref, gn_ref,
                   *refs, sections, k_scale, n_cast):
    cast_in, out_refs, cast_out = refs[:n_cast], refs[n_cast:-n_cast - 2], refs[-n_cast - 2:-2]
    h_ref, hn_ref = refs[-2:]
    for src, dst in zip(cast_in, cast_out):
        dst[...] = src[...].astype(BF16)

    @pl.when((pl.program_id(0) == 0) & (pl.program_id(1) == 0))
    def _():
        hn_ref[...] = _rms_modulate(x0_ref[0], gain_ref[...], sh0_ref[0], sc0_ref[0]).astype(BF16)

    h_ref[...] = hn_ref[...]
    hn_ref[...] = _rms_modulate(xn_ref[0], gain_ref[...], shn_ref[0], scn_ref[0]).astype(BF16)
    h = h_ref[...]
    order = sorted(range(len(sections)), key=lambda s: sections[s][0] == "raw")
    for (kind, lo, hi), o_ref in ((sections[s], out_refs[s]) for s in order):
        for c0 in range(lo, hi, COL_TILE):
            acc = _dot(h, w_ref[:, c0:c0 + COL_TILE])
            if kind == "q":
                acc = _rope(acc, cos_ref[...], sin_ref[...])
            elif kind == "k":
                acc = _rope(acc * k_scale, cos_ref[...], sin_ref[...])
            elif kind == "silu_gain":
                acc = _silu(acc) * gn_ref[:, c0 - lo:c0 - lo + COL_TILE]
            elif kind == "sigmoid":
                acc = jax.nn.sigmoid(acc)
            o_ref[0, :, c0 - lo:c0 - lo + COL_TILE] = acc.astype(BF16)


def _in_proj(x, mod3, gain, w_in, cos_t, sin_t, gn_w, sections, k_scale, cast_weights):
    b, l, d = x.shape
    n = w_in.shape[1]
    tm = TOKEN_TILE
    tiles = l // tm
    steps = tiles * b
    n_cast_blocks = steps // 2
    cast_block = lambda i, bi: ((i * b + bi) // 2, 0)
    cast_specs = [pl.BlockSpec((w.shape[0] // n_cast_blocks, w.shape[1]), cast_block) for w in cast_weights]
    for w in cast_weights:
        assert w.shape[0] % (n_cast_blocks * 16) == 0
    next_b = lambda i, bi: (bi + 1) % b
    next_t = lambda i, bi: jnp.minimum(i + (bi + 1) // b, tiles - 1)
    return pl.pallas_call(
        functools.partial(_inproj_kernel, sections=sections, k_scale=k_scale, n_cast=len(cast_weights)),
        grid=(tiles, b),
        in_specs=[
            pl.BlockSpec((1, tm, d), lambda i, bi: (0, 0, 0)),
            pl.BlockSpec((1, tm, d), lambda i, bi: (next_b(i, bi), next_t(i, bi), 0)),
            pl.BlockSpec((1, 1, d), lambda i, bi: (0, 0, 0)),
            pl.BlockSpec((1, 1, d), lambda i, bi: (0, 0, 1)),
            pl.BlockSpec((1, 1, d), lambda i, bi: (next_b(i, bi), 0, 0)),
            pl.BlockSpec((1, 1, d), lambda i, bi: (next_b(i, bi), 0, 1)),
            pl.BlockSpec((1, d), lambda i, bi: (0, 0)),
            pl.BlockSpec((d, n), lambda i, bi: (0, 0), pipeline_mode=pl.Buffered(1)),
            pl.BlockSpec((tm, LANES), lambda i, bi: (i, 0)),
            pl.BlockSpec((tm, LANES), lambda i, bi: (i, 0)),
            pl.BlockSpec(gn_w.shape, lambda i, bi: (0, 0)),
        ] + cast_specs,
        out_specs=[pl.BlockSpec((1, tm, hi - lo), lambda i, bi: (bi, i, 0)) for _, lo, hi in sections] + cast_specs,
        out_shape=[jax.ShapeDtypeStruct((b, l, hi - lo), BF16) for _, lo, hi in sections]
        + [jax.ShapeDtypeStruct(w.shape, BF16) for w in cast_weights],
        scratch_shapes=[pltpu.VMEM((tm, d), BF16), pltpu.VMEM((tm, d), BF16)],
        compiler_params=_params("arbitrary", "arbitrary"),
        name="in_proj",
    )(x, x, mod3, mod3, mod3, mod3, gain, w_in, cos_t, sin_t, gn_w, *cast_weights)


def _pool_kernel(u_ref, acol_ref, invc_ref, wpool_ref, scale_ref, o_ref, cm_ref, d_ref):
    l = u_ref.shape[1]
    rows = l // GRID_W
    for g, w in enumerate(POOL_WINDOWS):
        gs = slice(LANES * g, LANES * (g + 1))
        for t0 in range(0, l, POOL_BLOCK):
            cs = _dot(acol_ref[g], u_ref[0, t0:t0 + POOL_BLOCK, gs])
            cm_ref[t0:t0 + POOL_BLOCK, :] = cs * invc_ref[g]
        for r in range(rows):
            lo = max(r - w // 2, 0)
            hi = min(r + w - w // 2, rows)
            acc = cm_ref[GRID_W * lo:GRID_W * (lo + 1), :]
            for rr in range(lo + 1, hi):
                acc = acc + cm_ref[GRID_W * rr:GRID_W * (rr + 1), :]
            m = acc / float(hi - lo)
            rs = slice(GRID_W * r, GRID_W * (r + 1))
            d_ref[rs, :] = (m - u_ref[0, rs, gs].astype(F32)).astype(BF16)
        y = _dot(d_ref[...], wpool_ref[g]) * scale_ref[:, gs]
        o_ref[0, :, gs] = y.astype(BF16)


def _pool(u, acol, invc, w_pool, pool_scale):
    b, l, width = u.shape
    gd = w_pool.shape[1]
    return pl.pallas_call(
        _pool_kernel,
        grid=(b,),
        in_specs=[
            pl.BlockSpec((1, l, width), lambda i: (i, 0, 0)),
            pl.BlockSpec(acol.shape, lambda i: (0, 0, 0)),
            pl.BlockSpec(invc.shape, lambda i: (0, 0, 0)),
            pl.BlockSpec(w_pool.shape, lambda i: (0, 0, 0)),
            pl.BlockSpec((1, width), lambda i: (0, 0)),
        ],
        out_specs=pl.BlockSpec((1, l, width), lambda i: (i, 0, 0)),
        out_shape=jax.ShapeDtypeStruct((b, l, width), BF16),
        scratch_shapes=[pltpu.VMEM((l, gd), F32), pltpu.VMEM((l, gd), BF16)],
        compiler_params=_params("arbitrary"),
        name="pool",
    )(u, acol, invc, w_pool, pool_scale)


def _ret_kernel(q_ref, k_ref, v_ref, st_ref, decf_ref, decb_ref, o_ref, a_ref, rhs_ref, lhs_ref):
    c = RET_CHUNK
    l = q_ref.shape[1]
    n = l // c
    half = LANES // 2
    lane = lax.broadcasted_iota(jnp.int32, (1, LANES), 1)
    lgf0, lgf1 = _log_sigmoid(decf_ref[0]), _log_sigmoid(decf_ref[1])
    lgb0, lgb1 = _log_sigmoid(decb_ref[0]), _log_sigmoid(decb_ref[1])
    lgf = jnp.where(lane < half, lgf0, lgf1)
    lgb = jnp.where(lane < half, lgb0, lgb1)
    idx = lax.broadcasted_iota(jnp.int32, (c, LANES), 0).astype(F32)
    kdec_f = jnp.exp(lgf * (c - 1.0 - idx))
    kdec_b = jnp.exp(lgb * idx)
    qdec_f = jnp.exp(lgf * (idx + 1.0))
    qdec_b = jnp.exp(lgb * (c - idx))
    diff = idx - lax.broadcasted_iota(jnp.int32, (c, c), 1).astype(F32)

    def intra_mask(lf, lb):
        return jnp.where(diff >= 0.0, jnp.exp(lf * jnp.maximum(diff, 0.0)), jnp.exp(lb * jnp.maximum(-diff, 0.0)))

    mcat = jnp.concatenate([intra_mask(lgf0, lgb0), intra_mask(lgf1, lgb1)], axis=1)

    @pl.when((pl.program_id(0) == 0) & (pl.program_id(1) == 0))
    def _():
        rhs_ref[...] = jnp.zeros(rhs_ref.shape, BF16)

    head_rows = (slice(0, half), slice(half, LANES))
    head_cols = (slice(0, LANES), slice(LANES, 2 * LANES))

    for j in range(n):
        cs = slice(c * j, c * (j + 1))
        kj = k_ref[0, cs, :].astype(F32)
        kk = jnp.concatenate([kj * kdec_f, kj * kdec_b], axis=1).astype(BF16)
        a_ref[j] = lax.dot_general(kk, v_ref[0, cs, :], (((0,), (0,)), ((), ())), preferred_element_type=F32)
        for hh in range(2):
            rhs_ref[j, c * hh:c * (hh + 1), head_cols[hh]] = v_ref[0, cs, head_cols[hh]]

    for hh, (lf, lb) in enumerate(((lgf0, lgb0), (lgf1, lgb1))):
        rows, cols = head_rows[hh], head_cols[hh]
        sf = st_ref[0, 0, rows, cols]
        cdec = jnp.exp(lf * c)
        for j in range(n):
            rhs_ref[j, 2 * c + half * hh:2 * c + half * (hh + 1), cols] = sf.astype(BF16)
            sf = sf * cdec + a_ref[j, rows, cols]
        sb = st_ref[0, 0, LANES + half * hh:LANES + half * (hh + 1), cols]
        cdec = jnp.exp(lb * c)
        for j in reversed(range(n)):
            rhs_ref[j, 3 * c + half * hh:3 * c + half * (hh + 1), cols] = sb.astype(BF16)
            sb = sb * cdec + a_ref[j, LANES + half * hh:LANES + half * (hh + 1), cols]

    for j in range(n):
        cs = slice(c * j, c * (j + 1))
        qb = q_ref[0, cs, :]
        kb = k_ref[0, cs, :]
        zero_k = jnp.zeros_like(kb)
        kcat = jnp.concatenate([jnp.where(lane < half, kb, zero_k), jnp.where(lane >= half, kb, zero_k)], axis=0)
        scores = lax.dot_general(qb, kcat, (((1,), (1,)), ((), ())), preferred_element_type=F32)
        qf = qb.astype(F32)
        lhs_ref[j] = jnp.concatenate([scores * mcat, qf * qdec_f, qf * qdec_b], axis=1).astype(BF16)

    for j in range(n):
        o_ref[0, c * j:c * (j + 1), :] = _dot(lhs_ref[j], rhs_ref[j]).astype(BF16)


def _retention(q, k, v, states, decf, decb):
    b, l, v_width = v.shape
    pairs = RET_HEADS // 2
    n = l // RET_CHUNK
    return pl.pallas_call(
        _ret_kernel,
        grid=(b, pairs),
        in_specs=[
            pl.BlockSpec((1, l, LANES), lambda i, p: (i, 0, p)),
            pl.BlockSpec((1, l, LANES), lambda i, p: (i, 0, p)),
            pl.BlockSpec((1, l, 2 * LANES), lambda i, p: (i, 0, p)),
            pl.BlockSpec((1, 1, 2 * LANES, 2 * LANES), lambda i, p: (i, p, 0, 0)),
            pl.BlockSpec((2, 1, LANES), lambda i, p: (p, 0, 0)),
            pl.BlockSpec((2, 1, LANES), lambda i, p: (p, 0, 0)),
        ],
        out_specs=pl.BlockSpec((1, l, 2 * LANES), lambda i, p: (i, 0, p)),
        out_shape=jax.ShapeDtypeStruct((b, l, v_width), BF16),
        scratch_shapes=[pltpu.VMEM((n, 2 * LANES, 2 * LANES), F32),
                        pltpu.VMEM((n, 4 * RET_CHUNK, 2 * LANES), BF16),
                        pltpu.VMEM((n, RET_CHUNK, 4 * RET_CHUNK), BF16)],
        compiler_params=_params("arbitrary", "arbitrary"),
        name="retention",
    )(q, k, v, states, decf, decb)


def _tail_kernel(yp_ref, yr_ref, sg_ref, ga_ref, gb_ref, x_ref, gm_ref, sh_ref, sc_ref, gf_ref, gain_ref, gfin_ref,
                 wpa_ref, wrb_ref, wo_ref, w1_ref, w3_ref, w2_ref, o_ref, act_ref):
    a = _dot(yp_ref[0], wpa_ref[...])
    parts = []
    for hh in range(RET_HEADS):
        hs = slice(LANES * hh, LANES * (hh + 1))
        yh = yr_ref[0, :, hs].astype(F32)
        yc = yh - jnp.mean(yh, axis=-1, keepdims=True)
        var = jnp.mean(yc * yc, axis=-1, keepdims=True)
        parts.append((yc * lax.rsqrt(var + EPS)).astype(BF16) * sg_ref[0, :, hs])
    bb = _dot(jnp.concatenate(parts, axis=1), wrb_ref[...])
    m = (ga_ref[0].astype(F32) * a + gb_ref[0].astype(F32) * bb).astype(BF16)
    x1 = x_ref[0] + gm_ref[0] * _dot(m, wo_ref[...])
    h = _rms_modulate(x1, gain_ref[...], sh_ref[0], sc_ref[0]).astype(BF16)
    f = w1_ref.shape[1]
    for c0 in range(0, f, FFN_TILE):
        u1 = _dot(h, w1_ref[:, c0:c0 + FFN_TILE])
        u3 = _dot(h, w3_ref[:, c0:c0 + FFN_TILE])
        act_ref[:, c0:c0 + FFN_TILE] = (_silu(u1) * u3).astype(BF16)
    x2 = x1 + gf_ref[0] * _dot(act_ref[...], w2_ref[...])
    o_ref[0] = x2 * lax.rsqrt(jnp.mean(x2 * x2, axis=-1, keepdims=True) + EPS) * gfin_ref[...]


def _tail(yp, yr, sg, ga, gb, x, mod3, gain, gfin, w_pa, w_rb, w_o, w1, w3, w2):
    b, l, d = x.shape
    f = w1.shape[1]
    tm = TOKEN_TILE
    const = lambda shape: pl.BlockSpec(shape, lambda i, t: (0, 0), pipeline_mode=pl.Buffered(1))
    tile = lambda width: pl.BlockSpec((1, tm, width), lambda i, t: (i, t, 0))
    mod_row = lambda section: pl.BlockSpec((1, 1, d), lambda i, t: (i, 0, section))
    return pl.pallas_call(
        _tail_kernel,
        grid=(b, l // tm),
        in_specs=[
            tile(yp.shape[2]), tile(yr.shape[2]), tile(sg.shape[2]), tile(d), tile(d), tile(d),
            mod_row(2), mod_row(3), mod_row(4), mod_row(5),
            const((1, d)), const((1, d)),
            const(w_pa.shape), const(w_rb.shape), const(w_o.shape), const(w1.shape), const(w3.shape), const(w2.shape),
        ],
        out_specs=tile(d),
        out_shape=jax.ShapeDtypeStruct((b, l, d), F32),
        scratch_shapes=[pltpu.VMEM((tm, f), BF16)],
        compiler_params=_params("arbitrary", "arbitrary"),
        name="tail",
    )(yp, yr, sg, ga, gb, x, mod3, mod3, mod3, mod3, gain, gfin, w_pa, w_rb, w_o, w1, w3, w2)


def _rope_tables(l, qk_dim):
    t = np.arange(l)
    n_freq = qk_dim // 4
    inv_freq = ROPE_BASE ** (-np.arange(n_freq, dtype=np.float64) / n_freq)
    ang = np.concatenate([(t // GRID_W)[:, None] * inv_freq, (t % GRID_W)[:, None] * inv_freq], axis=-1)
    cos, sin = np.cos(ang), np.sin(ang)
    reps = LANES // qk_dim
    cos_t = np.tile(np.concatenate([cos, cos], axis=-1), (1, reps))
    sin_t = np.tile(np.concatenate([-sin, sin], axis=-1), (1, reps))
    return jnp.asarray(cos_t, dtype=F32), jnp.asarray(sin_t, dtype=F32)


def _pool_tables():
    t = np.arange(POOL_BLOCK)
    same_row = (t[:, None] // GRID_W) == (t[None, :] // GRID_W)
    off = t[None, :] - t[:, None]
    col = t % GRID_W
    acol, invc = [], []
    for w in POOL_WINDOWS:
        acol.append((same_row & (off >= -(w // 2)) & (off < w - w // 2)).astype(np.float32))
        cnt = np.minimum(col + w - w // 2, GRID_W) - np.maximum(col - w // 2, 0)
        invc.append(np.broadcast_to((1.0 / cnt.astype(np.float64)).astype(np.float32)[:, None], (POOL_BLOCK, LANES)))
    return jnp.asarray(np.stack(acol), dtype=BF16), jnp.asarray(np.stack(invc), dtype=F32)


def kernel(x, c, ctx, c_ctx, w_ada, b_ada, norm_mix, norm_ffn, w_in, w_pool, pool_scale,
           ret_decay_f, ret_decay_b, ret_gn_w, w_pa, w_rb, w_o, w_ff1, w_ff3, w_ff2, norm_final):
    assert w_ada.shape[0] == 1, "single-layer block"
    b, l, d = x.shape
    pool_width = w_pa.shape[1]
    v_width = w_rb.shape[1]
    qk_width = (w_in.shape[2] - pool_width - 2 * v_width - 2 * d) // 2
    qk_dim = qk_width // RET_HEADS
    k_scale = float(qk_dim) ** -0.5
    q_off = pool_width
    k_off = q_off + qk_width
    v_off = k_off + qk_width
    g_off = v_off + v_width
    ga_off = g_off + v_width
    gb_off = ga_off + d
    assert b < MOD_ROWS // 2 + 1 and qk_dim == LANES // 2 and v_width == RET_HEADS * LANES and qk_width == COL_TILE
    assert k_off % COL_TILE == 0 and g_off - k_off == 3 * COL_TILE
    sections = (("raw", 0, q_off), ("q", q_off, k_off), ("k", k_off, v_off), ("raw", v_off, g_off),
                ("silu_gain", g_off, ga_off), ("sigmoid", ga_off, gb_off), ("sigmoid", gb_off, gb_off + d))

    cc = jnp.zeros((MOD_ROWS, d), F32).at[:b].set(c).at[MOD_ROWS // 2].set(c_ctx)
    mod = _modulation(cc, w_ada[0], b_ada[0])
    mod3 = mod.reshape(MOD_ROWS, 1, 6 * d)

    decf = jnp.broadcast_to(ret_decay_f[0][:, None, None], (RET_HEADS, 1, LANES))
    decb = jnp.broadcast_to(ret_decay_b[0][:, None, None], (RET_HEADS, 1, LANES))
    gain_mix = norm_mix[0].reshape(1, d)

    states, w_in_b = _ctx_states(ctx, mod3, gain_mix, w_in[0], k_off // COL_TILE, decf, decb, k_scale)

    cos_t, sin_t = _rope_tables(l, qk_dim)
    (u, q, k, v, sg, ga, gb, w_pa_b, w_rb_b, w_o_b, w1_b, w3_b, w2_b) = _in_proj(
        x, mod3, gain_mix, w_in_b, cos_t, sin_t, ret_gn_w[0].reshape(1, v_width), sections, k_scale,
        (w_pa[0], w_rb[0], w_o[0], w_ff1[0], w_ff3[0], w_ff2[0]))

    acol, invc = _pool_tables()
    yp = _pool(u, acol, invc, w_pool[0].astype(BF16), pool_scale[0].reshape(1, pool_width))
    yr = _retention(q, k, v, states, decf, decb)
    return _tail(yp, yr, sg, ga, gb, x, mod3, norm_ffn[0].reshape(1, d), norm_final.reshape(1, d),
                 w_pa_b, w_rb_b, w_o_b, w1_b, w3_b, w2_b)
```

```python
import functools

import numpy as np
import jax
import jax.numpy as jnp
from jax import lax
from jax.experimental import pallas as pl
from jax.experimental.pallas import tpu as pltpu

F32 = jnp.float32
BF16 = jnp.bfloat16

GRID_W = 64
POOL_WINDOWS = (2, 4, 8, 16)
RET_HEADS = 8
RET_CHUNK = 128
ROPE_BASE = 10000.0
EPS = 1e-6

LANES = 128
MOD_ROWS = 16
VMEM_LIMIT_BYTES = 56 * 1024 * 1024

TOKEN_TILE = 512
COL_TILE = 512
POOL_BLOCK = 256
FFN_TILE = 256


def _silu(t):
    return t * jax.nn.sigmoid(t)


def _log_sigmoid(t):
    return jnp.minimum(t, 0.0) - jnp.log1p(jnp.exp(-jnp.abs(t)))


def _rms_modulate(x, gain, shift, scale):
    y = x * lax.rsqrt(jnp.mean(x * x, axis=-1, keepdims=True) + EPS) * gain
    return y * (1.0 + scale) + shift


def _dot(a, b):
    return jnp.dot(a, b, preferred_element_type=F32)


def _params(*semantics):
    return pltpu.CompilerParams(dimension_semantics=semantics, vmem_limit_bytes=VMEM_LIMIT_BYTES)


def _mod_kernel(c_ref, w_ref, b_ref, o_ref):
    s = _silu(c_ref[...]).astype(BF16)
    o_ref[...] = _dot(s, w_ref[...].astype(BF16)) + b_ref[...]


def _modulation(cc, w_ada, b_ada):
    d, n = w_ada.shape
    tn = d
    return pl.pallas_call(
        _mod_kernel,
        grid=(n // tn,),
        in_specs=[
            pl.BlockSpec((MOD_ROWS, d), lambda j: (0, 0)),
            pl.BlockSpec((d, tn), lambda j: (0, j)),
            pl.BlockSpec((1, tn), lambda j: (0, j)),
        ],
        out_specs=pl.BlockSpec((MOD_ROWS, tn), lambda j: (0, j)),
        out_shape=jax.ShapeDtypeStruct((MOD_ROWS, n), F32),
        compiler_params=_params("arbitrary"),
        name="modulation",
    )(cc, w_ada, b_ada.reshape(1, n))


def _ctx_kernel(ctx_ref, sh_ref, sc_ref, gain_ref, wk_ref, wv0_ref, wv1_ref, decf_ref, decb_ref, wrow_ref,
                o_ref, wrow_out_ref, wkv_ref, *, k_scale):
    lc = ctx_ref.shape[1]
    qk_width = wk_ref.shape[1]
    wrow_out_ref[...] = wrow_ref[...].astype(BF16)

    @pl.when(pl.program_id(0) == 0)
    def _():
        for i, w_ref in enumerate((wk_ref, wv0_ref, wv1_ref)):
            wkv_ref[:, qk_width * i:qk_width * (i + 1)] = w_ref[...].astype(BF16)

    h = _rms_modulate(ctx_ref[0], gain_ref[...], sh_ref[0], sc_ref[0]).astype(BF16)
    kv = _dot(h, wkv_ref[...])
    pos = lax.broadcasted_iota(jnp.int32, (lc, LANES), 0).astype(F32)
    lane = lax.broadcasted_iota(jnp.int32, (1, LANES), 1)
    row = lax.broadcasted_iota(jnp.int32, (2 * LANES, 2 * LANES), 0)
    col = lax.broadcasted_iota(jnp.int32, (2 * LANES, 2 * LANES), 1)
    same_head = ((row % LANES) < LANES // 2) == (col < LANES)
    for p in range(RET_HEADS // 2):
        k = kv[:, LANES * p:LANES * (p + 1)] * k_scale
        v = kv[:, qk_width + 2 * LANES * p:qk_width + 2 * LANES * (p + 1)].astype(BF16)
        lgf = jnp.where(lane < LANES // 2, _log_sigmoid(decf_ref[2 * p]), _log_sigmoid(decf_ref[2 * p + 1]))
        lgb = jnp.where(lane < LANES // 2, _log_sigmoid(decb_ref[2 * p]), _log_sigmoid(decb_ref[2 * p + 1]))
        kf = k * jnp.exp(lgf * (lc - 1.0 - pos))
        kb = k * jnp.exp(lgb * pos)
        kk = jnp.concatenate([kf, kb], axis=1).astype(BF16)
        st = lax.dot_general(kk, v, (((0,), (0,)), ((), ())), preferred_element_type=F32)
        o_ref[0, p] = jnp.where(same_head, st, 0.0)


def _ctx_states(ctx, mod3, gain, w_in, k_blk, decf, decb, k_scale):
    b, lc, d = ctx.shape
    n = w_in.shape[1]
    pairs = RET_HEADS // 2
    ctx_row = MOD_ROWS // 2
    wblock = lambda j: pl.BlockSpec((d, COL_TILE), lambda i: (0, k_blk + j), pipeline_mode=pl.Buffered(1))
    assert d % (b * 16) == 0
    wrow = pl.BlockSpec((d // b, n), lambda i: (i, 0))
    return pl.pallas_call(
        functools.partial(_ctx_kernel, k_scale=k_scale),
        grid=(b,),
        in_specs=[
            pl.BlockSpec((1, lc, d), lambda i: (i, 0, 0)),
            pl.BlockSpec((1, 1, d), lambda i: (ctx_row, 0, 0)),
            pl.BlockSpec((1, 1, d), lambda i: (ctx_row, 0, 1)),
            pl.BlockSpec((1, d), lambda i: (0, 0)),
            wblock(0), wblock(1), wblock(2),
            pl.BlockSpec((RET_HEADS, 1, LANES), lambda i: (0, 0, 0)),
            pl.BlockSpec((RET_HEADS, 1, LANES), lambda i: (0, 0, 0)),
            wrow,
        ],
        out_specs=[pl.BlockSpec((1, pairs, 2 * LANES, 2 * LANES), lambda i: (i, 0, 0, 0)), wrow],
        out_shape=[jax.ShapeDtypeStruct((b, pairs, 2 * LANES, 2 * LANES), F32), jax.ShapeDtypeStruct((d, n), BF16)],
        scratch_shapes=[pltpu.VMEM((d, 3 * COL_TILE), BF16)],
        compiler_params=_params("arbitrary"),
        name="ctx_states",
    )(ctx, mod3, mod3, gain, w_in, w_in, w_in, decf, decb, w_in)


def _rope(t, cos, sin_signed):
    lane = lax.broadcasted_iota(jnp.int32, (1, LANES), 1)
    first_half = (lane % (LANES // 2)) < LANES // 4
    parts = []
    for i in range(t.shape[1] // LANES):
        s = t[:, LANES * i:LANES * (i + 1)]
        ahead = pltpu.roll(s, LANES - LANES // 4, axis=1)
        behind = pltpu.roll(s, LANES // 4, axis=1)
        parts.append(s * cos + jnp.where(first_half, ahead, behind) * sin_signed)
    return jnp.concatenate(parts, axis=1)


def _inproj_kernel(x0_ref, xn_ref, sh0_ref, sc0_ref, shn_ref, scn_ref, gain_ref, w_ref, cos_ref, sin_ref, gn_ref,
                   *refs, sections, k_scale, n_cast):
    cast_in, out_refs, cast_out = refs[:n_cast], refs[n_cast:-n_cast - 2], refs[-n_cast - 2:-2]
    h_ref, hn_ref = refs[-2:]
    for src, dst in zip(cast_in, cast_out):
        dst[...] = src[...].astype(BF16)

    @pl.when((pl.program_id(0) == 0) & (pl.program_id(1) == 0))
    def _():
        hn_ref[...] = _rms_modulate(x0_ref[0], gain_ref[...], sh0_ref[0], sc0_ref[0]).astype(BF16)

    h_ref[...] = hn_ref[...]
    hn_ref[...] = _rms_modulate(xn_ref[0], gain_ref[...], shn_ref[0], scn_ref[0]).astype(BF16)
    h = h_ref[...]
    order = sorted(range(len(sections)), key=lambda s: sections[s][0] == "raw")
    for (kind, lo, hi), o_ref in ((sections[s], out_refs[s]) for s in order):
        for c0 in range(lo, hi, COL_TILE):
            acc = _dot(h, w_ref[:, c0:c0 + COL_TILE])
            if kind == "q":
                acc = _rope(acc, cos_ref[...], sin_ref[...])
            elif kind == "k":
                acc = _rope(acc * k_scale, cos_ref[...], sin_ref[...])
            elif kind == "silu_gain":
                acc = _silu(acc) * gn_ref[:, c0 - lo:c0 - lo + COL_TILE]
            elif kind == "sigmoid":
                acc = jax.nn.sigmoid(acc)
            o_ref[0, :, c0 - lo:c0 - lo + COL_TILE] = acc.astype(BF16)


def _in_proj(x, mod3, gain, w_in, cos_t, sin_t, gn_w, sections, k_scale, cast_weights):
    b, l, d = x.shape
    n = w_in.shape[1]
    tm = TOKEN_TILE
    tiles = l // tm
    steps = tiles * b
    n_cast_blocks = steps // 2
    cast_block = lambda i, bi: ((i * b + bi) // 2, 0)
    cast_specs = [pl.BlockSpec((w.shape[0] // n_cast_blocks, w.shape[1]), cast_block) for w in cast_weights]
    for w in cast_weights:
        assert w.shape[0] % (n_cast_blocks * 16) == 0
    next_b = lambda i, bi: (bi + 1) % b
    next_t = lambda i, bi: jnp.minimum(i + (bi + 1) // b, tiles - 1)
    return pl.pallas_call(
        functools.partial(_inproj_kernel, sections=sections, k_scale=k_scale, n_cast=len(cast_weights)),
        grid=(tiles, b),
        in_specs=[
            pl.BlockSpec((1, tm, d), lambda i, bi: (0, 0, 0)),
            pl.BlockSpec((1, tm, d), lambda i, bi: (next_b(i, bi), next_t(i, bi), 0)),
            pl.BlockSpec((1, 1, d), lambda i, bi: (0, 0, 0)),
            pl.BlockSpec((1, 1, d), lambda i, bi: (0, 0, 1)),
            pl.BlockSpec((1, 1, d), lambda i, bi: (next_b(i, bi), 0, 0)),
            pl.BlockSpec((1, 1, d), lambda i, bi: (next_b(i, bi), 0, 1)),
            pl.BlockSpec((1, d), lambda i, bi: (0, 0)),
            pl.BlockSpec((d, n), lambda i, bi: (0, 0), pipeline_mode=pl.Buffered(1)),
            pl.BlockSpec((tm, LANES), lambda i, bi: (i, 0)),
            pl.BlockSpec((tm, LANES), lambda i, bi: (i, 0)),
            pl.BlockSpec(gn_w.shape, lambda i, bi: (0, 0)),
        ] + cast_specs,
        out_specs=[pl.BlockSpec((1, tm, hi - lo), lambda i, bi: (bi, i, 0)) for _, lo, hi in sections] + cast_specs,
        out_shape=[jax.ShapeDtypeStruct((b, l, hi - lo), BF16) for _, lo, hi in sections]
        + [jax.ShapeDtypeStruct(w.shape, BF16) for w in cast_weights],
        scratch_shapes=[pltpu.VMEM((tm, d), BF16), pltpu.VMEM((tm, d), BF16)],
        compiler_params=_params("arbitrary", "arbitrary"),
        name="in_proj",
    )(x, x, mod3, mod3, mod3, mod3, gain, w_in, cos_t, sin_t, gn_w, *cast_weights)


def _pool_kernel(u_ref, acol_ref, invc_ref, wpool_ref, scale_ref, o_ref, cm_ref, d_ref):
    l = u_ref.shape[1]
    rows = l // GRID_W
    for g, w in enumerate(POOL_WINDOWS):
        gs = slice(LANES * g, LANES * (g + 1))
        for t0 in range(0, l, POOL_BLOCK):
            cs = _dot(acol_ref[g], u_ref[0, t0:t0 + POOL_BLOCK, gs])
            cm_ref[t0:t0 + POOL_BLOCK, :] = cs * invc_ref[g]
        for r in range(rows):
            lo = max(r - w // 2, 0)
            hi = min(r + w - w // 2, rows)
            acc = cm_ref[GRID_W * lo:GRID_W * (lo + 1), :]
            for rr in range(lo + 1, hi):
                acc = acc + cm_ref[GRID_W * rr:GRID_W * (rr + 1), :]
            m = acc / float(hi - lo)
            rs = slice(GRID_W * r, GRID_W * (r + 1))
            d_ref[rs, :] = (m - u_ref[0, rs, gs].astype(F32)).astype(BF16)
        y = _dot(d_ref[...], wpool_ref[g]) * scale_ref[:, gs]
        o_ref[0, :, gs] = y.astype(BF16)


def _pool(u, acol, invc, w_pool, pool_scale):
    b, l, width = u.shape
    gd = w_pool.shape[1]
    return pl.pallas_call(
        _pool_kernel,
        grid=(b,),
        in_specs=[
            pl.BlockSpec((1, l, width), lambda i: (i, 0, 0)),
            pl.BlockSpec(acol.shape, lambda i: (0, 0, 0)),
            pl.BlockSpec(invc.shape, lambda i: (0, 0, 0)),
            pl.BlockSpec(w_pool.shape, lambda i: (0, 0, 0)),
            pl.BlockSpec((1, width), lambda i: (0, 0)),
        ],
        out_specs=pl.BlockSpec((1, l, width), lambda i: (i, 0, 0)),
        out_shape=jax.ShapeDtypeStruct((b, l, width), BF16),
        scratch_shapes=[pltpu.VMEM((l, gd), F32), pltpu.VMEM((l, gd), BF16)],
        compiler_params=_params("arbitrary"),
        name="pool",
    )(u, acol, invc, w_pool, pool_scale)


def _ret_kernel(q_ref, k_ref, v_ref, st_ref, decf_ref, decb_ref, o_ref, a_ref, rhs_ref, lhs_ref, fence_sem):
    @pl.when(pl.program_id(0) == 0)
    def _():
        rhs_ref[...] = jnp.zeros(rhs_ref.shape, BF16)

    for p in range(RET_HEADS // 2):
        _ret_pair(q_ref, k_ref, v_ref, st_ref, decf_ref, decb_ref, o_ref, a_ref, rhs_ref, lhs_ref, p)
        pl.semaphore_signal(fence_sem, 1)
        pl.semaphore_wait(fence_sem, 1)


def _ret_pair(q_ref, k_ref, v_ref, st_ref, decf_ref, decb_ref, o_ref, a_ref, rhs_ref, lhs_ref, p):
    c = RET_CHUNK
    l = q_ref.shape[1]
    n = l // c
    half = LANES // 2
    qs = slice(LANES * p, LANES * (p + 1))
    vs = slice(2 * LANES * p, 2 * LANES * (p + 1))
    lane = lax.broadcasted_iota(jnp.int32, (1, LANES), 1)
    lgf0, lgf1 = _log_sigmoid(decf_ref[2 * p]), _log_sigmoid(decf_ref[2 * p + 1])
    lgb0, lgb1 = _log_sigmoid(decb_ref[2 * p]), _log_sigmoid(decb_ref[2 * p + 1])
    lgf = jnp.where(lane < half, lgf0, lgf1)
    lgb = jnp.where(lane < half, lgb0, lgb1)
    idx = lax.broadcasted_iota(jnp.int32, (c, LANES), 0).astype(F32)
    kdec_f = jnp.exp(lgf * (c - 1.0 - idx))
    kdec_b = jnp.exp(lgb * idx)
    qdec_f = jnp.exp(lgf * (idx + 1.0))
    qdec_b = jnp.exp(lgb * (c - idx))
    diff = idx - lax.broadcasted_iota(jnp.int32, (c, c), 1).astype(F32)

    def intra_mask(lf, lb):
        return jnp.where(diff >= 0.0, jnp.exp(lf * jnp.maximum(diff, 0.0)), jnp.exp(lb * jnp.maximum(-diff, 0.0)))

    mcat = jnp.concatenate([intra_mask(lgf0, lgb0), intra_mask(lgf1, lgb1)], axis=1)

    head_rows = (slice(0, half), slice(half, LANES))
    head_cols = (slice(0, LANES), slice(LANES, 2 * LANES))

    for j in range(n):
        cs = slice(c * j, c * (j + 1))
        kj = k_ref[0, cs, qs].astype(F32)
        kk = jnp.concatenate([kj * kdec_f, kj * kdec_b], axis=1).astype(BF16)
        a_ref[j] = lax.dot_general(kk, v_ref[0, cs, vs], (((0,), (0,)), ((), ())), preferred_element_type=F32)
        for hh in range(2):
            v_head = slice(vs.start + LANES * hh, vs.start + LANES * (hh + 1))
            rhs_ref[j, c * hh:c * (hh + 1), head_cols[hh]] = v_ref[0, cs, v_head]

    for hh, (lf, lb) in enumerate(((lgf0, lgb0), (lgf1, lgb1))):
        rows, cols = head_rows[hh], head_cols[hh]
        sf = st_ref[0, p, rows, cols]
        cdec = jnp.exp(lf * c)
        for j in range(n):
            rhs_ref[j, 2 * c + half * hh:2 * c + half * (hh + 1), cols] = sf.astype(BF16)
            sf = sf * cdec + a_ref[j, rows, cols]
        sb = st_ref[0, p, LANES + half * hh:LANES + half * (hh + 1), cols]
        cdec = jnp.exp(lb * c)
        for j in reversed(range(n)):
            rhs_ref[j, 3 * c + half * hh:3 * c + half * (hh + 1), cols] = sb.astype(BF16)
            sb = sb * cdec + a_ref[j, LANES + half * hh:LANES + half * (hh + 1), cols]

    for j in range(n):
        cs = slice(c * j, c * (j + 1))
        qb = q_ref[0, cs, qs]
        kb = k_ref[0, cs, qs]
        zero_k = jnp.zeros_like(kb)
        kcat = jnp.concatenate([jnp.where(lane < half, kb, zero_k), jnp.where(lane >= half, kb, zero_k)], axis=0)
        scores = lax.dot_general(qb, kcat, (((1,), (1,)), ((), ())), preferred_element_type=F32)
        qf = qb.astype(F32)
        lhs_ref[j] = jnp.concatenate([scores * mcat, qf * qdec_f, qf * qdec_b], axis=1).astype(BF16)

    for j in range(n):
        o_ref[0, c * j:c * (j + 1), vs] = _dot(lhs_ref[j], rhs_ref[j]).astype(BF16)


def _retention(q, k, v, states, decf, decb):
    b, l, v_width = v.shape
    pairs = RET_HEADS // 2
    n = l // RET_CHUNK
    return pl.pallas_call(
        _ret_kernel,
        grid=(b,),
        in_specs=[
            pl.BlockSpec((1, l, q.shape[2]), lambda i: (i, 0, 0)),
            pl.BlockSpec((1, l, k.shape[2]), lambda i: (i, 0, 0)),
            pl.BlockSpec((1, l, v_width), lambda i: (i, 0, 0)),
            pl.BlockSpec((1, pairs, 2 * LANES, 2 * LANES), lambda i: (i, 0, 0, 0)),
            pl.BlockSpec((RET_HEADS, 1, LANES), lambda i: (0, 0, 0)),
            pl.BlockSpec((RET_HEADS, 1, LANES), lambda i: (0, 0, 0)),
        ],
        out_specs=pl.BlockSpec((1, l, v_width), lambda i: (i, 0, 0)),
        out_shape=jax.ShapeDtypeStruct((b, l, v_width), BF16),
        scratch_shapes=[pltpu.VMEM((n, 2 * LANES, 2 * LANES), F32),
                        pltpu.VMEM((n, 4 * RET_CHUNK, 2 * LANES), BF16),
                        pltpu.VMEM((n, RET_CHUNK, 4 * RET_CHUNK), BF16),
                        pltpu.SemaphoreType.REGULAR],
        compiler_params=_params("arbitrary"),
        name="retention",
    )(q, k, v, states, decf, decb)


def _tail_kernel(yp_ref, yr_ref, sg_ref, ga_ref, gb_ref, x_ref, gm_ref, sh_ref, sc_ref, gf_ref, gain_ref, gfin_ref,
                 wpa_ref, wrb_ref, wo_ref, w1_ref, w3_ref, w2_ref, o_ref, act_ref):
    a = _dot(yp_ref[0], wpa_ref[...])
    parts = []
    for hh in range(RET_HEADS):
        hs = slice(LANES * hh, LANES * (hh + 1))
        yh = yr_ref[0, :, hs].astype(F32)
        yc = yh - jnp.mean(yh, axis=-1, keepdims=True)
        var = jnp.mean(yc * yc, axis=-1, keepdims=True)
        parts.append((yc * lax.rsqrt(var + EPS)).astype(BF16) * sg_ref[0, :, hs])
    bb = _dot(jnp.concatenate(parts, axis=1), wrb_ref[...])
    m = (ga_ref[0].astype(F32) * a + gb_ref[0].astype(F32) * bb).astype(BF16)
    x1 = x_ref[0] + gm_ref[0] * _dot(m, wo_ref[...])
    h = _rms_modulate(x1, gain_ref[...], sh_ref[0], sc_ref[0]).astype(BF16)
    f = w1_ref.shape[1]
    for c0 in range(0, f, FFN_TILE):
        u1 = _dot(h, w1_ref[:, c0:c0 + FFN_TILE])
        u3 = _dot(h, w3_ref[:, c0:c0 + FFN_TILE])
        act_ref[:, c0:c0 + FFN_TILE] = (_silu(u1) * u3).astype(BF16)
    x2 = x1 + gf_ref[0] * _dot(act_ref[...], w2_ref[...])
    o_ref[0] = x2 * lax.rsqrt(jnp.mean(x2 * x2, axis=-1, keepdims=True) + EPS) * gfin_ref[...]


def _tail(yp, yr, sg, ga, gb, x, mod3, gain, gfin, w_pa, w_rb, w_o, w1, w3, w2):
    b, l, d = x.shape
    f = w1.shape[1]
    tm = TOKEN_TILE
    const = lambda shape: pl.BlockSpec(shape, lambda i, t: (0, 0), pipeline_mode=pl.Buffered(1))
    tile = lambda width: pl.BlockSpec((1, tm, width), lambda i, t: (i, t, 0))
    mod_row = lambda section: pl.BlockSpec((1, 1, d), lambda i, t: (i, 0, section))
    return pl.pallas_call(
        _tail_kernel,
        grid=(b, l // tm),
        in_specs=[
            tile(yp.shape[2]), tile(yr.shape[2]), tile(sg.shape[2]), tile(d), tile(d), tile(d),
            mod_row(2), mod_row(3), mod_row(4), mod_row(5),
            const((1, d)), const((1, d)),
            const(w_pa.shape), const(w_rb.shape), const(w_o.shape), const(w1.shape), const(w3.shape), const(w2.shape),
        ],
        out_specs=tile(d),
        out_shape=jax.ShapeDtypeStruct((b, l, d), F32),
        scratch_shapes=[pltpu.VMEM((tm, f), BF16)],
        compiler_params=_params("arbitrary", "arbitrary"),
        name="tail",
    )(yp, yr, sg, ga, gb, x, mod3, mod3, mod3, mod3, gain, gfin, w_pa, w_rb, w_o, w1, w3, w2)


def _rope_tables(l, qk_dim):
    t = np.arange(l)
    n_freq = qk_dim // 4
    inv_freq = ROPE_BASE ** (-np.arange(n_freq, dtype=np.float64) / n_freq)
    ang = np.concatenate([(t // GRID_W)[:, None] * inv_freq, (t % GRID_W)[:, None] * inv_freq], axis=-1)
    cos, sin = np.cos(ang), np.sin(ang)
    reps = LANES // qk_dim
    cos_t = np.tile(np.concatenate([cos, cos], axis=-1), (1, reps))
    sin_t = np.tile(np.concatenate([-sin, sin], axis=-1), (1, reps))
    return jnp.asarray(cos_t, dtype=F32), jnp.asarray(sin_t, dtype=F32)


def _pool_tables():
    t = np.arange(POOL_BLOCK)
    same_row = (t[:, None] // GRID_W) == (t[None, :] // GRID_W)
    off = t[None, :] - t[:, None]
    col = t % GRID_W
    acol, invc = [], []
    for w in POOL_WINDOWS:
        acol.append((same_row & (off >= -(w // 2)) & (off < w - w // 2)).astype(np.float32))
        cnt = np.minimum(col + w - w // 2, GRID_W) - np.maximum(col - w // 2, 0)
        invc.append(np.broadcast_to((1.0 / cnt.astype(np.float64)).astype(np.float32)[:, None], (POOL_BLOCK, LANES)))
    return jnp.asarray(np.stack(acol), dtype=BF16), jnp.asarray(np.stack(invc), dtype=F32)


def kernel(x, c, ctx, c_ctx, w_ada, b_ada, norm_mix, norm_ffn, w_in, w_pool, pool_scale,
           ret_decay_f, ret_decay_b, ret_gn_w, w_pa, w_rb, w_o, w_ff1, w_ff3, w_ff2, norm_final):
    assert w_ada.shape[0] == 1, "single-layer block"
    b, l, d = x.shape
    pool_width = w_pa.shape[1]
    v_width = w_rb.shape[1]
    qk_width = (w_in.shape[2] - pool_width - 2 * v_width - 2 * d) // 2
    qk_dim = qk_width // RET_HEADS
    k_scale = float(qk_dim) ** -0.5
    q_off = pool_width
    k_off = q_off + qk_width
    v_off = k_off + qk_width
    g_off = v_off + v_width
    ga_off = g_off + v_width
    gb_off = ga_off + d
    assert b < MOD_ROWS // 2 + 1 and qk_dim == LANES // 2 and v_width == RET_HEADS * LANES and qk_width == COL_TILE
    assert k_off % COL_TILE == 0 and g_off - k_off == 3 * COL_TILE
    sections = (("raw", 0, q_off), ("q", q_off, k_off), ("k", k_off, v_off), ("raw", v_off, g_off),
                ("silu_gain", g_off, ga_off), ("sigmoid", ga_off, gb_off), ("sigmoid", gb_off, gb_off + d))

    cc = jnp.zeros((MOD_ROWS, d), F32).at[:b].set(c).at[MOD_ROWS // 2].set(c_ctx)
    mod = _modulation(cc, w_ada[0], b_ada[0])
    mod3 = mod.reshape(MOD_ROWS, 1, 6 * d)

    decf = jnp.broadcast_to(ret_decay_f[0][:, None, None], (RET_HEADS, 1, LANES))
    decb = jnp.broadcast_to(ret_decay_b[0][:, None, None], (RET_HEADS, 1, LANES))
    gain_mix = norm_mix[0].reshape(1, d)

    states, w_in_b = _ctx_states(ctx, mod3, gain_mix, w_in[0], k_off // COL_TILE, decf, decb, k_scale)

    cos_t, sin_t = _rope_tables(l, qk_dim)
    (u, q, k, v, sg, ga, gb, w_pa_b, w_rb_b, w_o_b, w1_b, w3_b, w2_b) = _in_proj(
        x, mod3, gain_mix, w_in_b, cos_t, sin_t, ret_gn_w[0].reshape(1, v_width), sections, k_scale,
        (w_pa[0], w_rb[0], w_o[0], w_ff1[0], w_ff3[0], w_ff2[0]))

    acol, invc = _pool_tables()
    yp = _pool(u, acol, invc, w_pool[0].astype(BF16), pool_scale[0].reshape(1, pool_width))
    yr = _retention(q, k, v, states, decf, decb)
    return _tail(yp, yr, sg, ga, gb, x, mod3, norm_ffn[0].reshape(1, d), norm_final.reshape(1, d),
                 w_pa_b, w_rb_b, w_o_b, w1_b, w3_b, w2_b)
```

```python
import functools

import numpy as np
import jax
import jax.numpy as jnp
from jax import lax
from jax.experimental import pallas as pl
from jax.experimental.pallas import tpu as pltpu

F32 = jnp.float32
BF16 = jnp.bfloat16

GRID_W = 64
POOL_WINDOWS = (2, 4, 8, 16)
RET_HEADS = 8
RET_CHUNK = 128
ROPE_BASE = 10000.0
EPS = 1e-6

LANES = 128
MOD_ROWS = 16
VMEM_LIMIT_BYTES = 56 * 1024 * 1024

TOKEN_TILE = 512
COL_TILE = 512
PIECE = 256
POOL_BLOCK = 256
FFN_TILE = 256


def _sigmoid(t):
    return 0.5 * jnp.tanh(0.5 * t) + 0.5


def _silu(t):
    return t * _sigmoid(t)


def _log_sigmoid(t):
    return jnp.minimum(t, 0.0) - jnp.log1p(jnp.exp(-jnp.abs(t)))


def _rms_modulate(x, gain, shift, scale):
    y = x * lax.rsqrt(jnp.mean(x * x, axis=-1, keepdims=True) + EPS) * gain
    return y * (1.0 + scale) + shift


def _dot(a, b):
    return jnp.dot(a, b, preferred_element_type=F32)


def _params(*semantics):
    return pltpu.CompilerParams(dimension_semantics=semantics, vmem_limit_bytes=VMEM_LIMIT_BYTES)


def _mod_kernel(c_ref, w_ref, b_ref, o_ref):
    s = _silu(c_ref[...]).astype(BF16)
    o_ref[...] = _dot(s, w_ref[...].astype(BF16)) + b_ref[...]


def _modulation(cc, w_ada, b_ada):
    d, n = w_ada.shape
    tn = d
    return pl.pallas_call(
        _mod_kernel,
        grid=(n // tn,),
        in_specs=[
            pl.BlockSpec((MOD_ROWS, d), lambda j: (0, 0)),
            pl.BlockSpec((d, tn), lambda j: (0, j)),
            pl.BlockSpec((1, tn), lambda j: (0, j)),
        ],
        out_specs=pl.BlockSpec((MOD_ROWS, tn), lambda j: (0, j)),
        out_shape=jax.ShapeDtypeStruct((MOD_ROWS, n), F32),
        compiler_params=_params("arbitrary"),
        name="modulation",
    )(cc, w_ada, b_ada.reshape(1, n))


def _ctx_kernel(ctx_ref, sh_ref, sc_ref, gain_ref, wk_ref, wv0_ref, wv1_ref, decf_ref, decb_ref, wrow_ref,
                o_ref, wrow_out_ref, wkv_ref, *, k_scale):
    lc = ctx_ref.shape[1]
    qk_width = wk_ref.shape[1]
    wrow_out_ref[...] = wrow_ref[...].astype(BF16)

    @pl.when(pl.program_id(0) == 0)
    def _():
        for i, w_ref in enumerate((wk_ref, wv0_ref, wv1_ref)):
            wkv_ref[:, qk_width * i:qk_width * (i + 1)] = w_ref[...].astype(BF16)

    h = _rms_modulate(ctx_ref[0], gain_ref[...], sh_ref[0], sc_ref[0]).astype(BF16)
    kv = _dot(h, wkv_ref[...])
    pos = lax.broadcasted_iota(jnp.int32, (lc, LANES), 0).astype(F32)
    lane = lax.broadcasted_iota(jnp.int32, (1, LANES), 1)
    row = lax.broadcasted_iota(jnp.int32, (2 * LANES, 2 * LANES), 0)
    col = lax.broadcasted_iota(jnp.int32, (2 * LANES, 2 * LANES), 1)
    same_head = ((row % LANES) < LANES // 2) == (col < LANES)
    for p in range(RET_HEADS // 2):
        k = kv[:, LANES * p:LANES * (p + 1)] * k_scale
        v = kv[:, qk_width + 2 * LANES * p:qk_width + 2 * LANES * (p + 1)].astype(BF16)
        lgf = jnp.where(lane < LANES // 2, _log_sigmoid(decf_ref[2 * p]), _log_sigmoid(decf_ref[2 * p + 1]))
        lgb = jnp.where(lane < LANES // 2, _log_sigmoid(decb_ref[2 * p]), _log_sigmoid(decb_ref[2 * p + 1]))
        kf = k * jnp.exp(lgf * (lc - 1.0 - pos))
        kb = k * jnp.exp(lgb * pos)
        kk = jnp.concatenate([kf, kb], axis=1).astype(BF16)
        st = lax.dot_general(kk, v, (((0,), (0,)), ((), ())), preferred_element_type=F32)
        o_ref[0, p] = jnp.where(same_head, st, 0.0)


def _ctx_states(ctx, mod3, gain, w_in, k_blk, decf, decb, k_scale):
    b, lc, d = ctx.shape
    n = w_in.shape[1]
    pairs = RET_HEADS // 2
    ctx_row = MOD_ROWS // 2
    wblock = lambda j: pl.BlockSpec((d, COL_TILE), lambda i: (0, k_blk + j), pipeline_mode=pl.Buffered(1))
    assert d % (b * 16) == 0
    wrow = pl.BlockSpec((d // b, n), lambda i: (i, 0))
    return pl.pallas_call(
        functools.partial(_ctx_kernel, k_scale=k_scale),
        grid=(b,),
        in_specs=[
            pl.BlockSpec((1, lc, d), lambda i: (i, 0, 0)),
            pl.BlockSpec((1, 1, d), lambda i: (ctx_row, 0, 0)),
            pl.BlockSpec((1, 1, d), lambda i: (ctx_row, 0, 1)),
            pl.BlockSpec((1, d), lambda i: (0, 0)),
            wblock(0), wblock(1), wblock(2),
            pl.BlockSpec((RET_HEADS, 1, LANES), lambda i: (0, 0, 0)),
            pl.BlockSpec((RET_HEADS, 1, LANES), lambda i: (0, 0, 0)),
            wrow,
        ],
        out_specs=[pl.BlockSpec((1, pairs, 2 * LANES, 2 * LANES), lambda i: (i, 0, 0, 0)), wrow],
        out_shape=[jax.ShapeDtypeStruct((b, pairs, 2 * LANES, 2 * LANES), F32), jax.ShapeDtypeStruct((d, n), BF16)],
        scratch_shapes=[pltpu.VMEM((d, 3 * COL_TILE), BF16)],
        compiler_params=_params("arbitrary"),
        name="ctx_states",
    )(ctx, mod3, mod3, gain, w_in, w_in, w_in, decf, decb, w_in)


def _rope(t, cos, sin_signed):
    lane = lax.broadcasted_iota(jnp.int32, (1, LANES), 1)
    first_half = (lane % (LANES // 2)) < LANES // 4
    parts = []
    for i in range(t.shape[1] // LANES):
        s = t[:, LANES * i:LANES * (i + 1)]
        ahead = pltpu.roll(s, LANES - LANES // 4, axis=1)
        behind = pltpu.roll(s, LANES // 4, axis=1)
        parts.append(s * cos + jnp.where(first_half, ahead, behind) * sin_signed)
    return jnp.concatenate(parts, axis=1)


def _inproj_kernel(x0_ref, xn_ref, sh0_ref, sc0_ref, shn_ref, scn_ref, gain_ref, w_ref, cos_ref, sin_ref, gn_ref,
                   *refs, sections, k_scale, n_cast):
    cast_in, out_refs, cast_out = refs[:n_cast], refs[n_cast:-n_cast - 2], refs[-n_cast - 2:-2]
    h_ref, hn_ref = refs[-2:]
    for src, dst in zip(cast_in, cast_out):
        dst[...] = src[...].astype(BF16)

    @pl.when((pl.program_id(0) == 0) & (pl.program_id(1) == 0))
    def _():
        hn_ref[...] = _rms_modulate(x0_ref[0], gain_ref[...], sh0_ref[0], sc0_ref[0]).astype(BF16)

    h_ref[...] = hn_ref[...]
    hn_ref[...] = _rms_modulate(xn_ref[0], gain_ref[...], shn_ref[0], scn_ref[0]).astype(BF16)
    h = h_ref[...]
    pieces = [(kind, lo, c0, o_ref) for (kind, lo, hi), o_ref in zip(sections, out_refs) for c0 in range(lo, hi, PIECE)]
    heavy = [pc for pc in pieces if pc[0] in ("silu_gain", "sigmoid")]
    light = sorted((pc for pc in pieces if pc[0] not in ("silu_gain", "sigmoid")), key=lambda pc: pc[0] == "raw")
    extra = max(len(heavy) - len(light), 0)
    order, heavy = heavy[:extra], heavy[extra:]
    while heavy or light:
        order += heavy[:1] + light[:1]
        heavy, light = heavy[1:], light[1:]
    for kind, lo, c0, o_ref in order:
        acc = _dot(h, w_ref[:, c0:c0 + PIECE])
        if kind == "q":
            acc = _rope(acc, cos_ref[...], sin_ref[...])
        elif kind == "k":
            acc = _rope(acc * k_scale, cos_ref[...], sin_ref[...])
        elif kind == "silu_gain":
            acc = _silu(acc) * gn_ref[:, c0 - lo:c0 - lo + PIECE]
        elif kind == "sigmoid":
            acc = _sigmoid(acc)
        o_ref[0, :, c0 - lo:c0 - lo + PIECE] = acc.astype(BF16)


def _in_proj(x, mod3, gain, w_in, cos_t, sin_t, gn_w, sections, k_scale, cast_weights):
    b, l, d = x.shape
    n = w_in.shape[1]
    tm = TOKEN_TILE
    tiles = l // tm
    steps = tiles * b
    n_cast_blocks = steps // 2
    cast_block = lambda i, bi: ((i * b + bi) // 2, 0)
    cast_specs = [pl.BlockSpec((w.shape[0] // n_cast_blocks, w.shape[1]), cast_block) for w in cast_weights]
    for w in cast_weights:
        assert w.shape[0] % (n_cast_blocks * 16) == 0
    next_b = lambda i, bi: (bi + 1) % b
    next_t = lambda i, bi: jnp.minimum(i + (bi + 1) // b, tiles - 1)
    return pl.pallas_call(
        functools.partial(_inproj_kernel, sections=sections, k_scale=k_scale, n_cast=len(cast_weights)),
        grid=(tiles, b),
        in_specs=[
            pl.BlockSpec((1, tm, d), lambda i, bi: (0, 0, 0)),
            pl.BlockSpec((1, tm, d), lambda i, bi: (next_b(i, bi), next_t(i, bi), 0)),
            pl.BlockSpec((1, 1, d), lambda i, bi: (0, 0, 0)),
            pl.BlockSpec((1, 1, d), lambda i, bi: (0, 0, 1)),
            pl.BlockSpec((1, 1, d), lambda i, bi: (next_b(i, bi), 0, 0)),
            pl.BlockSpec((1, 1, d), lambda i, bi: (next_b(i, bi), 0, 1)),
            pl.BlockSpec((1, d), lambda i, bi: (0, 0)),
            pl.BlockSpec((d, n), lambda i, bi: (0, 0), pipeline_mode=pl.Buffered(1)),
            pl.BlockSpec((tm, LANES), lambda i, bi: (i, 0)),
            pl.BlockSpec((tm, LANES), lambda i, bi: (i, 0)),
            pl.BlockSpec(gn_w.shape, lambda i, bi: (0, 0)),
        ] + cast_specs,
        out_specs=[pl.BlockSpec((1, tm, hi - lo), lambda i, bi: (bi, i, 0)) for _, lo, hi in sections] + cast_specs,
        out_shape=[jax.ShapeDtypeStruct((b, l, hi - lo), BF16) for _, lo, hi in sections]
        + [jax.ShapeDtypeStruct(w.shape, BF16) for w in cast_weights],
        scratch_shapes=[pltpu.VMEM((tm, d), BF16), pltpu.VMEM((tm, d), BF16)],
        compiler_params=_params("arbitrary", "arbitrary"),
        name="in_proj",
    )(x, x, mod3, mod3, mod3, mod3, gain, w_in, cos_t, sin_t, gn_w, *cast_weights)


def _pool_kernel(u_ref, acol_ref, invc_ref, wpool_ref, scale_ref, o_ref, cm_ref, d_ref):
    l = u_ref.shape[1]
    rows = l // GRID_W
    for g, w in enumerate(POOL_WINDOWS):
        gs = slice(LANES * g, LANES * (g + 1))
        for t0 in range(0, l, POOL_BLOCK):
            cs = _dot(acol_ref[g], u_ref[0, t0:t0 + POOL_BLOCK, gs])
            cm_ref[t0:t0 + POOL_BLOCK, :] = cs * invc_ref[g]
        for r in range(rows):
            lo = max(r - w // 2, 0)
            hi = min(r + w - w // 2, rows)
            acc = cm_ref[GRID_W * lo:GRID_W * (lo + 1), :]
            for rr in range(lo + 1, hi):
                acc = acc + cm_ref[GRID_W * rr:GRID_W * (rr + 1), :]
            m = acc / float(hi - lo)
            rs = slice(GRID_W * r, GRID_W * (r + 1))
            d_ref[rs, :] = (m - u_ref[0, rs, gs].astype(F32)).astype(BF16)
        y = _dot(d_ref[...], wpool_ref[g]) * scale_ref[:, gs]
        o_ref[0, :, gs] = y.astype(BF16)


def _pool(u, acol, invc, w_pool, pool_scale):
    b, l, width = u.shape
    gd = w_pool.shape[1]
    return pl.pallas_call(
        _pool_kernel,
        grid=(b,),
        in_specs=[
            pl.BlockSpec((1, l, width), lambda i: (i, 0, 0)),
            pl.BlockSpec(acol.shape, lambda i: (0, 0, 0)),
            pl.BlockSpec(invc.shape, lambda i: (0, 0, 0)),
            pl.BlockSpec(w_pool.shape, lambda i: (0, 0, 0)),
            pl.BlockSpec((1, width), lambda i: (0, 0)),
        ],
        out_specs=pl.BlockSpec((1, l, width), lambda i: (i, 0, 0)),
        out_shape=jax.ShapeDtypeStruct((b, l, width), BF16),
        scratch_shapes=[pltpu.VMEM((l, gd), F32), pltpu.VMEM((l, gd), BF16)],
        compiler_params=_params("arbitrary"),
        name="pool",
    )(u, acol, invc, w_pool, pool_scale)


def _ret_kernel(q_ref, k_ref, v_ref, st_ref, decf_ref, decb_ref, o_ref, a_ref, rhs_ref, lhs_ref):
    @pl.when((pl.program_id(0) == 0) & (pl.program_id(1) == 0))
    def _():
        rhs_ref[...] = jnp.zeros(rhs_ref.shape, BF16)

    _ret_pair(q_ref, k_ref, v_ref, st_ref, decf_ref, decb_ref, o_ref, a_ref, rhs_ref, lhs_ref, 0)


def _ret_pair(q_ref, k_ref, v_ref, st_ref, decf_ref, decb_ref, o_ref, a_ref, rhs_ref, lhs_ref, p):
    c = RET_CHUNK
    l = q_ref.shape[1]
    n = l // c
    half = LANES // 2
    qs = slice(LANES * p, LANES * (p + 1))
    vs = slice(2 * LANES * p, 2 * LANES * (p + 1))
    lane = lax.broadcasted_iota(jnp.int32, (1, LANES), 1)
    lgf0, lgf1 = _log_sigmoid(decf_ref[2 * p]), _log_sigmoid(decf_ref[2 * p + 1])
    lgb0, lgb1 = _log_sigmoid(decb_ref[2 * p]), _log_sigmoid(decb_ref[2 * p + 1])
    lgf = jnp.where(lane < half, lgf0, lgf1)
    lgb = jnp.where(lane < half, lgb0, lgb1)
    idx = lax.broadcasted_iota(jnp.int32, (c, LANES), 0).astype(F32)
    kdec_f = jnp.exp(lgf * (c - 1.0 - idx))
    kdec_b = jnp.exp(lgb * idx)
    qdec_f = jnp.exp(lgf * (idx + 1.0))
    qdec_b = jnp.exp(lgb * (c - idx))
    diff = idx - lax.broadcasted_iota(jnp.int32, (c, c), 1).astype(F32)

    def intra_mask(lf, lb):
        return jnp.where(diff >= 0.0, jnp.exp(lf * jnp.maximum(diff, 0.0)), jnp.exp(lb * jnp.maximum(-diff, 0.0)))

    mcat = jnp.concatenate([intra_mask(lgf0, lgb0), intra_mask(lgf1, lgb1)], axis=1)

    head_rows = (slice(0, half), slice(half, LANES))
    head_cols = (slice(0, LANES), slice(LANES, 2 * LANES))

    for j in range(n):
        cs = slice(c * j, c * (j + 1))
        kj = k_ref[0, cs, qs].astype(F32)
        kk = jnp.concatenate([kj * kdec_f, kj * kdec_b], axis=1).astype(BF16)
        a_ref[j] = lax.dot_general(kk, v_ref[0, cs, vs], (((0,), (0,)), ((), ())), preferred_element_type=F32)
        for hh in range(2):
            v_head = slice(vs.start + LANES * hh, vs.start + LANES * (hh + 1))
            rhs_ref[j, c * hh:c * (hh + 1), head_cols[hh]] = v_ref[0, cs, v_head]

    for hh, (lf, lb) in enumerate(((lgf0, lgb0), (lgf1, lgb1))):
        rows, cols = head_rows[hh], head_cols[hh]
        sf = st_ref[0, p, rows, cols]
        cdec = jnp.exp(lf * c)
        for j in range(n):
            rhs_ref[j, 2 * c + half * hh:2 * c + half * (hh + 1), cols] = sf.astype(BF16)
            sf = sf * cdec + a_ref[j, rows, cols]
        sb = st_ref[0, p, LANES + half * hh:LANES + half * (hh + 1), cols]
        cdec = jnp.exp(lb * c)
        for j in reversed(range(n)):
            rhs_ref[j, 3 * c + half * hh:3 * c + half * (hh + 1), cols] = sb.astype(BF16)
            sb = sb * cdec + a_ref[j, LANES + half * hh:LANES + half * (hh + 1), cols]

    for j in range(n):
        cs = slice(c * j, c * (j + 1))
        qb = q_ref[0, cs, qs]
        kb = k_ref[0, cs, qs]
        zero_k = jnp.zeros_like(kb)
        kcat = jnp.concatenate([jnp.where(lane < half, kb, zero_k), jnp.where(lane >= half, kb, zero_k)], axis=0)
        scores = lax.dot_general(qb, kcat, (((1,), (1,)), ((), ())), preferred_element_type=F32)
        qf = qb.astype(F32)
        lhs_ref[j] = jnp.concatenate([scores * mcat, qf * qdec_f, qf * qdec_b], axis=1).astype(BF16)

    for j in range(n):
        o_ref[0, c * j:c * (j + 1), vs] = _dot(lhs_ref[j], rhs_ref[j]).astype(BF16)


def _retention(q, k, v, states, decf, decb):
    b, l, v_width = v.shape
    pairs = RET_HEADS // 2
    n = l // RET_CHUNK
    return pl.pallas_call(
        _ret_kernel,
        grid=(b, pairs),
        in_specs=[
            pl.BlockSpec((1, l, LANES), lambda i, p: (i, 0, p)),
            pl.BlockSpec((1, l, LANES), lambda i, p: (i, 0, p)),
            pl.BlockSpec((1, l, 2 * LANES), lambda i, p: (i, 0, p)),
            pl.BlockSpec((1, 1, 2 * LANES, 2 * LANES), lambda i, p: (i, p, 0, 0)),
            pl.BlockSpec((2, 1, LANES), lambda i, p: (p, 0, 0)),
            pl.BlockSpec((2, 1, LANES), lambda i, p: (p, 0, 0)),
        ],
        out_specs=pl.BlockSpec((1, l, 2 * LANES), lambda i, p: (i, 0, p)),
        out_shape=jax.ShapeDtypeStruct((b, l, v_width), BF16),
        scratch_shapes=[pltpu.VMEM((n, 2 * LANES, 2 * LANES), F32),
                        pltpu.VMEM((n, 4 * RET_CHUNK, 2 * LANES), BF16),
                        pltpu.VMEM((n, RET_CHUNK, 4 * RET_CHUNK), BF16)],
        compiler_params=_params("arbitrary", "arbitrary"),
        name="retention",
    )(q, k, v, states, decf, decb)


def _tail_kernel(yp_ref, yr_ref, sg_ref, ga_ref, gb_ref, x_ref, gm_ref, sh_ref, sc_ref, gf_ref, gain_ref, gfin_ref,
                 wpa_ref, wrb_ref, wo_ref, w1_ref, w3_ref, w2_ref, o_ref, act_ref):
    a = _dot(yp_ref[0], wpa_ref[...])
    parts = []
    for hh in range(RET_HEADS):
        hs = slice(LANES * hh, LANES * (hh + 1))
        yh = yr_ref[0, :, hs].astype(F32)
        yc = yh - jnp.mean(yh, axis=-1, keepdims=True)
        var = jnp.mean(yc * yc, axis=-1, keepdims=True)
        parts.append((yc * lax.rsqrt(var + EPS)).astype(BF16) * sg_ref[0, :, hs])
    bb = _dot(jnp.concatenate(parts, axis=1), wrb_ref[...])
    m = (ga_ref[0].astype(F32) * a + gb_ref[0].astype(F32) * bb).astype(BF16)
    x1 = x_ref[0] + gm_ref[0] * _dot(m, wo_ref[...])
    h = _rms_modulate(x1, gain_ref[...], sh_ref[0], sc_ref[0]).astype(BF16)
    f = w1_ref.shape[1]
    for c0 in range(0, f, FFN_TILE):
        u1 = _dot(h, w1_ref[:, c0:c0 + FFN_TILE])
        u3 = _dot(h, w3_ref[:, c0:c0 + FFN_TILE])
        act_ref[:, c0:c0 + FFN_TILE] = (_silu(u1) * u3).astype(BF16)
    x2 = x1 + gf_ref[0] * _dot(act_ref[...], w2_ref[...])
    o_ref[0] = x2 * lax.rsqrt(jnp.mean(x2 * x2, axis=-1, keepdims=True) + EPS) * gfin_ref[...]


def _tail(yp, yr, sg, ga, gb, x, mod3, gain, gfin, w_pa, w_rb, w_o, w1, w3, w2):
    b, l, d = x.shape
    f = w1.shape[1]
    tm = TOKEN_TILE
    const = lambda shape: pl.BlockSpec(shape, lambda i, t: (0, 0), pipeline_mode=pl.Buffered(1))
    tile = lambda width: pl.BlockSpec((1, tm, width), lambda i, t: (i, t, 0))
    mod_row = lambda section: pl.BlockSpec((1, 1, d), lambda i, t: (i, 0, section))
    return pl.pallas_call(
        _tail_kernel,
        grid=(b, l // tm),
        in_specs=[
            tile(yp.shape[2]), tile(yr.shape[2]), tile(sg.shape[2]), tile(d), tile(d), tile(d),
            mod_row(2), mod_row(3), mod_row(4), mod_row(5),
            const((1, d)), const((1, d)),
            const(w_pa.shape), const(w_rb.shape), const(w_o.shape), const(w1.shape), const(w3.shape), const(w2.shape),
        ],
        out_specs=tile(d),
        out_shape=jax.ShapeDtypeStruct((b, l, d), F32),
        scratch_shapes=[pltpu.VMEM((tm, f), BF16)],
        compiler_params=_params("arbitrary", "arbitrary"),
        name="tail",
    )(yp, yr, sg, ga, gb, x, mod3, mod3, mod3, mod3, gain, gfin, w_pa, w_rb, w_o, w1, w3, w2)


def _rope_tables(l, qk_dim):
    t = np.arange(l)
    n_freq = qk_dim // 4
    inv_freq = ROPE_BASE ** (-np.arange(n_freq, dtype=np.float64) / n_freq)
    ang = np.concatenate([(t // GRID_W)[:, None] * inv_freq, (t % GRID_W)[:, None] * inv_freq], axis=-1)
    cos, sin = np.cos(ang), np.sin(ang)
    reps = LANES // qk_dim
    cos_t = np.tile(np.concatenate([cos, cos], axis=-1), (1, reps))
    sin_t = np.tile(np.concatenate([-sin, sin], axis=-1), (1, reps))
    return jnp.asarray(cos_t, dtype=F32), jnp.asarray(sin_t, dtype=F32)


def _pool_tables():
    t = np.arange(POOL_BLOCK)
    same_row = (t[:, None] // GRID_W) == (t[None, :] // GRID_W)
    off = t[None, :] - t[:, None]
    col = t % GRID_W
    acol, invc = [], []
    for w in POOL_WINDOWS:
        acol.append((same_row & (off >= -(w // 2)) & (off < w - w // 2)).astype(np.float32))
        cnt = np.minimum(col + w - w // 2, GRID_W) - np.maximum(col - w // 2, 0)
        invc.append(np.broadcast_to((1.0 / cnt.astype(np.float64)).astype(np.float32)[:, None], (POOL_BLOCK, LANES)))
    return jnp.asarray(np.stack(acol), dtype=BF16), jnp.asarray(np.stack(invc), dtype=F32)


def kernel(x, c, ctx, c_ctx, w_ada, b_ada, norm_mix, norm_ffn, w_in, w_pool, pool_scale,
           ret_decay_f, ret_decay_b, ret_gn_w, w_pa, w_rb, w_o, w_ff1, w_ff3, w_ff2, norm_final):
    assert w_ada.shape[0] == 1, "single-layer block"
    b, l, d = x.shape
    pool_width = w_pa.shape[1]
    v_width = w_rb.shape[1]
    qk_width = (w_in.shape[2] - pool_width - 2 * v_width - 2 * d) // 2
    qk_dim = qk_width // RET_HEADS
    k_scale = float(qk_dim) ** -0.5
    q_off = pool_width
    k_off = q_off + qk_width
    v_off = k_off + qk_width
    g_off = v_off + v_width
    ga_off = g_off + v_width
    gb_off = ga_off + d
    assert b < MOD_ROWS // 2 + 1 and qk_dim == LANES // 2 and v_width == RET_HEADS * LANES and qk_width == COL_TILE
    assert k_off % COL_TILE == 0 and g_off - k_off == 3 * COL_TILE
    sections = (("raw", 0, q_off), ("q", q_off, k_off), ("k", k_off, v_off), ("raw", v_off, g_off),
                ("silu_gain", g_off, ga_off), ("sigmoid", ga_off, gb_off), ("sigmoid", gb_off, gb_off + d))

    cc = jnp.zeros((MOD_ROWS, d), F32).at[:b].set(c).at[MOD_ROWS // 2].set(c_ctx)
    mod = _modulation(cc, w_ada[0], b_ada[0])
    mod3 = mod.reshape(MOD_ROWS, 1, 6 * d)

    decf = jnp.broadcast_to(ret_decay_f[0][:, None, None], (RET_HEADS, 1, LANES))
    decb = jnp.broadcast_to(ret_decay_b[0][:, None, None], (RET_HEADS, 1, LANES))
    gain_mix = norm_mix[0].reshape(1, d)

    states, w_in_b = _ctx_states(ctx, mod3, gain_mix, w_in[0], k_off // COL_TILE, decf, decb, k_scale)

    cos_t, sin_t = _rope_tables(l, qk_dim)
    (u, q, k, v, sg, ga, gb, w_pa_b, w_rb_b, w_o_b, w1_b, w3_b, w2_b) = _in_proj(
        x, mod3, gain_mix, w_in_b, cos_t, sin_t, ret_gn_w[0].reshape(1, v_width), sections, k_scale,
        (w_pa[0], w_rb[0], w_o[0], w_ff1[0], w_ff3[0], w_ff2[0]))

    acol, invc = _pool_tables()
    yp = _pool(u, acol, invc, w_pool[0].astype(BF16), pool_scale[0].reshape(1, pool_width))
    yr = _retention(q, k, v, states, decf, decb)
    return _tail(yp, yr, sg, ga, gb, x, mod3, norm_ffn[0].reshape(1, d), norm_final.reshape(1, d),
                 w_pa_b, w_rb_b, w_o_b, w1_b, w3_b, w2_b)
```
